```python
import math
import jax
import jax.numpy as jnp
from jax import lax
import numpy as np

D_MODEL = 1024
BATCH = 8
SEQ = 2048
DEPTH = 4
DEC_BATCH = 128
DEC_SEQ = 4
PAST_LEN = 2048
PAGE_SIZE = 128

N_META = 16
DA_HEADS = 8
DA_HEAD_DIM = D_MODEL // (2 * DA_HEADS)
DA_VALUE_DIM = 2 * DA_HEAD_DIM
QK_WIDTH = 2 * DA_HEADS * DA_HEAD_DIM
DA_WIDTH = DA_HEADS * DA_VALUE_DIM
ROPE_THETA = 10000.0
Q_BLOCK = 128
SUBLN_EPS = 1e-5
RW_HEAD = 64
RW_WIDTH = D_MODEL
RW_HEADS = RW_WIDTH // RW_HEAD
DECAY_LORA = 64
AAA_LORA = 64
MV_LORA = 32
GATE_LORA = 128
RW_COLS = 3 * RW_WIDTH + DECAY_LORA + AAA_LORA + GATE_LORA
LNX_EPS = 64e-5
D_FF = 4 * D_MODEL
IN_COLS = 2 * QK_WIDTH + DA_WIDTH + RW_COLS + 2 * D_MODEL

kernel_name = 'hybrid_diffattn_rwkv7_decoder_step'

F32 = jnp.float32


def rms_norm(x, g, eps=1e-6):
    xf = x.astype(F32)
    y = xf * lax.rsqrt(jnp.mean(xf * xf, axis=-1, keepdims=True) + eps)
    return (y * g.astype(F32)).astype(x.dtype)


def apply_rope(x, pos):
    d = x.shape[-1]
    half = d // 2
    inv_freq = ROPE_THETA ** (-jnp.arange(half, dtype=F32) * (2.0 / d))
    ang = pos.astype(F32)[:, None] * inv_freq[None, :]
    cos = jnp.cos(ang)[:, None, :]
    sin = jnp.sin(ang)[:, None, :]
    xf = x.astype(F32)
    x1, x2 = xf[..., :half], xf[..., half:]
    return jnp.concatenate([x1 * cos - x2 * sin, x2 * cos + x1 * sin], axis=-1).astype(x.dtype)


def diff_attend(q, qpos, k, v, kpos, lam):
    b, tq = q.shape[:2]
    tk = k.shape[1]
    qh = q.reshape(b, tq, DA_HEADS, 2, DA_HEAD_DIM)
    kh = k.reshape(b, tk, DA_HEADS, 2, DA_HEAD_DIM)
    s = jnp.einsum('bqhcd,bkhcd->bhcqk', qh, kh).astype(F32) * (DA_HEAD_DIM ** -0.5)
    s = jnp.where(kpos[None, :] <= qpos[:, None], s, jnp.finfo(F32).min)
    p = jax.nn.softmax(s, axis=-1)
    w = p[:, :, 0] - lam * p[:, :, 1]
    return jnp.einsum('bhqk,bkhe->bqhe', w.astype(v.dtype), v)


def diff_attend_prompt(q, k, v, lam):
    b, L = q.shape[:2]
    pos = jnp.arange(L)
    out_meta = diff_attend(q[:, :N_META], pos[:N_META], k[:, :N_META], v[:, :N_META], pos[:N_META], lam)
    n_blk = (L - N_META) // Q_BLOCK
    qb = jnp.moveaxis(q[:, N_META:].reshape(b, n_blk, Q_BLOCK, 2 * DA_HEADS, DA_HEAD_DIM), 1, 0)
    pb = pos[N_META:].reshape(n_blk, Q_BLOCK)
    out_b = lax.map(lambda qp: diff_attend(qp[0], qp[1], k, v, pos, lam), (qb, pb))
    out_b = jnp.moveaxis(out_b, 0, 1).reshape(b, L - N_META, DA_HEADS, DA_VALUE_DIM)
    return jnp.concatenate([out_meta, out_b], axis=1)


def wkv7_scan(S0, r, decay, k, v, a, bb):
    def step(S, inp):
        r_t, w_t, k_t, v_t, a_t, b_t = inp
        sa = jnp.einsum('bhij,bhj->bhi', S, a_t)
        S = S * w_t[:, :, None, :] + sa[..., None] * b_t[:, :, None, :] + v_t[..., None] * k_t[:, :, None, :]
        y = jnp.einsum('bhij,bhj->bhi', S, r_t)
        return S, y
    xs = tuple(jnp.moveaxis(t.astype(F32), 1, 0) for t in (r, decay, k, v, a, bb))
    S, ys = lax.scan(step, S0.astype(F32), xs)
    return S, jnp.moveaxis(ys, 0, 1)


def rwkv7_branch(zr, shift0, wkv0, v_first, l, p):
    b, t, _ = zr.shape
    C = RW_WIDTH
    prev = jnp.concatenate([shift0[:, None, :].astype(zr.dtype), zr[:, :-1]], axis=1)
    m = zr + (prev - zr) * p['mu']
    r = m[..., :C]
    k = m[..., C:2 * C]
    v = m[..., 2 * C:3 * C]
    o = 3 * C
    wd = m[..., o:o + DECAY_LORA]
    o += DECAY_LORA
    ad = m[..., o:o + AAA_LORA]
    o += AAA_LORA
    gd = m[..., o:o + GATE_LORA]
    w = -jax.nn.softplus(-(p['w0'] + jnp.tanh(wd) @ p['w2'])) - 0.5
    decay = jnp.exp(-jnp.exp(w.astype(F32)))
    if l == 0:
        v_first = v
    else:
        v = v + (v_first - v) * jax.nn.sigmoid(p['v0'] + (v @ p['v1']) @ p['v2'])
    a = jax.nn.sigmoid(p['a0'] + ad @ p['a2'])
    g = jax.nn.sigmoid(gd) @ p['g2']
    hs = lambda u: u.reshape(b, t, RW_HEADS, RW_HEAD)
    kk = hs(k * p['k_k']).astype(F32)
    kk = kk * lax.rsqrt(jnp.maximum(jnp.sum(kk * kk, axis=-1, keepdims=True), 1e-24))
    k = k * (1 + (a - 1) * p['k_a'])
    a_h = hs(a).astype(F32)
    rh, kh, vh = hs(r).astype(F32), hs(k).astype(F32), hs(v).astype(F32)
    S, y = wkv7_scan(wkv0, rh, hs(decay), kh, vh, -kk, kk * a_h)
    mu = jnp.mean(y, axis=-1, keepdims=True)
    var = jnp.mean(jnp.square(y - mu), axis=-1, keepdims=True)
    y = ((y - mu) * lax.rsqrt(var + LNX_EPS)).reshape(b, t, C) * p['ln_w'].astype(F32) + p['ln_b'].astype(F32)
    bonus = jnp.sum(rh * kh * p['r_k'].astype(F32), axis=-1, keepdims=True) * vh
    y = (y + bonus.reshape(b, t, C)) * g.astype(F32)
    return y.astype(zr.dtype), S.astype(wkv0.dtype), v_first


def token_mixer(h, pos, k_past, v_past, wkv0, shift0, v_first, l, lam, lam_init, p):
    b, t, _ = h.shape
    z = h @ p['w_in']
    o = 0
    q = z[..., o:o + QK_WIDTH].reshape(b, t, 2 * DA_HEADS, DA_HEAD_DIM)
    o += QK_WIDTH
    k = z[..., o:o + QK_WIDTH].reshape(b, t, 2 * DA_HEADS, DA_HEAD_DIM)
    o += QK_WIDTH
    v = z[..., o:o + DA_WIDTH].reshape(b, t, DA_HEADS, DA_VALUE_DIM)
    o += DA_WIDTH
    zr = z[..., o:o + RW_COLS]
    o += RW_COLS
    gate_a = z[..., o:o + D_MODEL]
    gate_b = z[..., o + D_MODEL:o + 2 * D_MODEL]
    q = apply_rope(q, pos)
    k = apply_rope(k, pos)
    if k_past is None:
        att = diff_attend_prompt(q, k, v, lam)
    else:
        K = jnp.concatenate([k_past.astype(k.dtype), k], axis=1)
        V = jnp.concatenate([v_past.astype(v.dtype), v], axis=1)
        att = diff_attend(q, pos, K, V, jnp.arange(K.shape[1]), lam)
    att = (rms_norm(att, p['subln'], SUBLN_EPS) * (1.0 - lam_init)).reshape(b, t, DA_WIDTH)
    rw, wkv, v_first = rwkv7_branch(zr, shift0, wkv0, v_first, l, p)
    new_shift = zr[:, -1]
    merged = jax.nn.sigmoid(gate_a) * (att @ p['w_ba']) + jax.nn.sigmoid(gate_b) * (rw @ p['w_bb'])
    return merged @ p['w_out'], k, v, wkv, new_shift, v_first


def decoder_layer(x, pos, k_past, v_past, wkv0, shift0, v_first, l, lam, lam_init, p):
    h = rms_norm(x, p['n1'])
    mix, k_new, v_new, wkv, shift, v_first = token_mixer(h, pos, k_past, v_past, wkv0, shift0, v_first, l, lam, lam_init, p)
    x = x + rms_norm(mix, p['n2'])
    h = rms_norm(x, p['n3'])
    f = jnp.square(jax.nn.relu(h @ p['w_up'])) @ p['w_down']
    x = x + rms_norm(f, p['n4'])
    return x, k_new, v_new, wkv, shift, v_first


def setup_inputs(seed: int = 0) -> dict:
    key = jax.random.key(seed)
    ks = iter(jax.random.split(key, 64))
    nrm = lambda shape, scale: jax.random.normal(next(ks), shape, F32) * scale
    gain = lambda shape: 1.0 + nrm(shape, 0.02)
    n_pages = PAST_LEN // PAGE_SIZE
    n_used = DEC_BATCH * n_pages
    n_pool = n_used + n_used // 4
    x_prompt = nrm((BATCH, SEQ, D_MODEL), 1.0)
    x_sample = nrm((DEC_BATCH, DEC_SEQ, D_MODEL), 1.0)
    cache_k = nrm((DEPTH, n_pool, PAGE_SIZE, 2 * DA_HEADS, DA_HEAD_DIM), 1.0)
    cache_v = nrm((DEPTH, n_pool, PAGE_SIZE, DA_HEADS, DA_VALUE_DIM), 1.0)
    state_wkv = nrm((DEPTH, DEC_BATCH, RW_HEADS, RW_HEAD, RW_HEAD), 0.3)
    state_shift = nrm((DEPTH, DEC_BATCH, RW_COLS), 1.0)
    page_table = jax.random.permutation(next(ks), n_pool)[:n_used].reshape(DEC_BATCH, n_pages).astype(jnp.int32)
    meta_tokens = nrm((N_META, D_MODEL), 1.0)
    norm_pre_mix = gain((DEPTH, D_MODEL))
    norm_post_mix = gain((DEPTH, D_MODEL))
    norm_pre_ffn = gain((DEPTH, D_MODEL))
    norm_post_ffn = gain((DEPTH, D_MODEL))
    w_in = nrm((DEPTH, D_MODEL, IN_COLS), D_MODEL ** -0.5)
    lambda_q1 = nrm((DEPTH, DA_HEAD_DIM), 0.1)
    lambda_k1 = nrm((DEPTH, DA_HEAD_DIM), 0.1)
    lambda_q2 = nrm((DEPTH, DA_HEAD_DIM), 0.1)
    lambda_k2 = nrm((DEPTH, DA_HEAD_DIM), 0.1)
    da_subln = gain((DEPTH, DA_VALUE_DIM))
    rw_mu = jax.random.uniform(next(ks), (DEPTH, RW_COLS), F32)
    rw_w0 = jax.random.uniform(next(ks), (DEPTH, RW_WIDTH), F32, minval=-6.0, maxval=-1.0)
    rw_w2 = nrm((DEPTH, DECAY_LORA, RW_WIDTH), 0.1)
    rw_a0 = nrm((DEPTH, RW_WIDTH), 0.1)
    rw_a2 = nrm((DEPTH, AAA_LORA, RW_WIDTH), 0.5 * AAA_LORA ** -0.5)
    rw_g2 = nrm((DEPTH, GATE_LORA, RW_WIDTH), GATE_LORA ** -0.5)
    rw_v0 = nrm((DEPTH - 1, RW_WIDTH), 0.1)
    rw_v1 = nrm((DEPTH - 1, RW_WIDTH, MV_LORA), RW_WIDTH ** -0.5)
    rw_v2 = nrm((DEPTH - 1, MV_LORA, RW_WIDTH), 0.5 * MV_LORA ** -0.5)
    rw_k_k = 0.85 + nrm((DEPTH, RW_WIDTH), 0.05)
    rw_k_a = 1.0 + nrm((DEPTH, RW_WIDTH), 0.05)
    rw_r_k = nrm((DEPTH, RW_HEADS, RW_HEAD), 0.1)
    rw_ln_w = gain((DEPTH, RW_WIDTH))
    rw_ln_b = nrm((DEPTH, RW_WIDTH), 0.01)
    w_branch_a = nrm((DEPTH, DA_WIDTH, D_MODEL), DA_WIDTH ** -0.5)
    w_branch_b = nrm((DEPTH, RW_WIDTH, D_MODEL), RW_WIDTH ** -0.5)
    w_out = nrm((DEPTH, D_MODEL, D_MODEL), D_MODEL ** -0.5)
    w_up = nrm((DEPTH, D_MODEL, D_FF), D_MODEL ** -0.5)
    w_down = nrm((DEPTH, D_FF, D_MODEL), D_FF ** -0.5)
    return {'x_prompt': x_prompt, 'x_sample': x_sample, 'cache_k': cache_k, 'cache_v': cache_v,
            'state_wkv': state_wkv, 'state_shift': state_shift, 'page_table': page_table,
            'meta_tokens': meta_tokens, 'norm_pre_mix': norm_pre_mix, 'norm_post_mix': norm_post_mix,
            'norm_pre_ffn': norm_pre_ffn, 'norm_post_ffn': norm_post_ffn, 'w_in': w_in,
            'lambda_q1': lambda_q1, 'lambda_k1': lambda_k1, 'lambda_q2': lambda_q2, 'lambda_k2': lambda_k2,
            'da_subln': da_subln, 'rw_mu': rw_mu, 'rw_w0': rw_w0, 'rw_w2': rw_w2, 'rw_a0': rw_a0,
            'rw_a2': rw_a2, 'rw_g2': rw_g2, 'rw_v0': rw_v0, 'rw_v1': rw_v1, 'rw_v2': rw_v2,
            'rw_k_k': rw_k_k, 'rw_k_a': rw_k_a, 'rw_r_k': rw_r_k, 'rw_ln_w': rw_ln_w, 'rw_ln_b': rw_ln_b,
            'w_branch_a': w_branch_a, 'w_branch_b': w_branch_b, 'w_out': w_out, 'w_up': w_up, 'w_down': w_down}


def reference(x_prompt, x_sample, cache_k, cache_v, state_wkv, state_shift, page_table,
              meta_tokens, norm_pre_mix, norm_post_mix, norm_pre_ffn, norm_post_ffn, w_in,
              lambda_q1, lambda_k1, lambda_q2, lambda_k2, da_subln, rw_mu, rw_w0, rw_w2, rw_a0,
              rw_a2, rw_g2, rw_v0, rw_v1, rw_v2, rw_k_k, rw_k_a, rw_r_k, rw_ln_w, rw_ln_b,
              w_branch_a, w_branch_b, w_out, w_up, w_down):
    b = x_prompt.shape[0]
    db, ds = x_sample.shape[:2]
    xp = jnp.concatenate([jnp.broadcast_to(meta_tokens.astype(x_prompt.dtype)[None], (b, N_META, D_MODEL)), x_prompt], axis=1)
    pos_p = jnp.arange(xp.shape[1])
    n_past = page_table.shape[1] * PAGE_SIZE
    pos_s = n_past + jnp.arange(ds)
    xs = x_sample
    wkv0_p = jnp.zeros((b, RW_HEADS, RW_HEAD, RW_HEAD), x_prompt.dtype)
    shift0_p = jnp.zeros((b, RW_COLS), x_prompt.dtype)
    vf_p = None
    vf_s = None
    kp_l, vp_l, wp_l, sp_l, ks_l, vs_l, ws_l, ss_l = [], [], [], [], [], [], [], []
    for l in range(DEPTH):
        p = {'n1': norm_pre_mix[l], 'n2': norm_post_mix[l], 'n3': norm_pre_ffn[l], 'n4': norm_post_ffn[l],
             'w_in': w_in[l], 'subln': da_subln[l], 'mu': rw_mu[l], 'w0': rw_w0[l], 'w2': rw_w2[l],
             'a0': rw_a0[l], 'a2': rw_a2[l], 'g2': rw_g2[l], 'k_k': rw_k_k[l], 'k_a': rw_k_a[l],
             'r_k': rw_r_k[l], 'ln_w': rw_ln_w[l], 'ln_b': rw_ln_b[l], 'w_ba': w_branch_a[l],
             'w_bb': w_branch_b[l], 'w_out': w_out[l], 'w_up': w_up[l], 'w_down': w_down[l]}
        if l > 0:
            p['v0'] = rw_v0[l - 1]
            p['v1'] = rw_v1[l - 1]
            p['v2'] = rw_v2[l - 1]
        lam_init = 0.8 - 0.6 * math.exp(-0.3 * l)
        lam = (jnp.exp(jnp.sum(lambda_q1[l].astype(F32) * lambda_k1[l].astype(F32)))
               - jnp.exp(jnp.sum(lambda_q2[l].astype(F32) * lambda_k2[l].astype(F32))) + lam_init)
        xp, k_new, v_new, wkv_new, sh_new, vf_p = decoder_layer(xp, pos_p, None, None, wkv0_p, shift0_p, vf_p, l, lam, lam_init, p)
        kp_l.append(k_new)
        vp_l.append(v_new)
        wp_l.append(wkv_new)
        sp_l.append(sh_new)
        k_past = cache_k[l, page_table].reshape(db, n_past, 2 * DA_HEADS, DA_HEAD_DIM)
        v_past = cache_v[l, page_table].reshape(db, n_past, DA_HEADS, DA_VALUE_DIM)
        xs, k_new, v_new, wkv_new, sh_new, vf_s = decoder_layer(xs, pos_s, k_past, v_past, state_wkv[l], state_shift[l], vf_s, l, lam, lam_init, p)
        ks_l.append(k_new)
        vs_l.append(v_new)
        ws_l.append(wkv_new)
        ss_l.append(sh_new)
    y_prompt = xp[:, N_META:]
    y_sample = xs
    new_k_prompt = jnp.stack(kp_l)
    new_v_prompt = jnp.stack(vp_l)
    new_wkv_prompt = jnp.stack(wp_l)
    new_shift_prompt = jnp.stack(sp_l)
    new_k_sample = jnp.stack(ks_l)
    new_v_sample = jnp.stack(vs_l)
    new_wkv_sample = jnp.stack(ws_l)
    new_shift_sample = jnp.stack(ss_l)
    return (y_prompt, y_sample, new_k_prompt, new_v_prompt, new_wkv_prompt, new_shift_prompt, new_k_sample, new_v_sample, new_wkv_sample, new_shift_sample)
```

```python
import functools
import math

import jax
import jax.numpy as jnp
from jax import lax
from jax.experimental import pallas as pl
from jax.experimental.pallas import tpu as pltpu

F32 = jnp.float32
BF16 = jnp.bfloat16

D_MODEL = 1024
N_META = 16
DA_HEADS = 8
DA_HEAD_DIM = 64
DA_VALUE_DIM = 128
ROPE_THETA = 10000.0
SUBLN_EPS = 1e-5
RW_HEAD = 64
RW_HEADS = 16
RW_PAIRS = RW_HEADS // 2
RW_WIDTH = 1024
DECAY_LORA = 64
AAA_LORA = 64
GATE_LORA = 128
MV_LORA = 32
RW_COLS = 3 * RW_WIDTH + DECAY_LORA + AAA_LORA + GATE_LORA
LNX_EPS = 64e-5
PAGE_SIZE = 128
LANES = 128
NEG = -1e30
VMEM_LIMIT = 52 * 1024 * 1024

NT_DIMS = (((1,), (1,)), ((), ()))


def _bdot(a, b):
    return jnp.dot(a.astype(BF16), b.astype(BF16), preferred_element_type=F32)


def _bdot_nt(a, b):
    return lax.dot_general(a.astype(BF16), b.astype(BF16), NT_DIMS, preferred_element_type=F32)


def _sigmoid(x):
    return 1.0 / (1.0 + jnp.exp(-x))


def _params(sem):
    return pltpu.CompilerParams(dimension_semantics=sem, vmem_limit_bytes=VMEM_LIMIT)


def _norm_matmul_kernel(x_ref, g_ref, w_ref, *rest, rope, scale):
    if rope:
        cos_ref, sa_ref, sb_ref, o_ref = rest
    else:
        (o_ref,) = rest
    x = x_ref[0]
    ms = jnp.mean(x * x, axis=-1, keepdims=True)
    h = (x * lax.rsqrt(ms + 1e-6) * g_ref[...]).astype(BF16)
    acc = jnp.dot(h, w_ref[...], preferred_element_type=F32)
    if rope:
        cos, sa, sb = cos_ref[...], sa_ref[...], sb_ref[...]
        for c in range(acc.shape[1] // LANES):
            blk = acc[:, c * LANES:(c + 1) * LANES]
            out = blk * cos + pltpu.roll(blk, 96, 1) * sa + pltpu.roll(blk, 32, 1) * sb
            if scale != 1.0:
                out = out * scale
            o_ref[0, :, c * LANES:(c + 1) * LANES] = out.astype(o_ref.dtype)
    else:
        o_ref[0] = acc.astype(o_ref.dtype)


def _norm_matmul(x3, g, w, *, tm, tn, out_dtype, rope_tabs=None, scale=1.0):
    bv, lv, d = x3.shape
    n = w.shape[1]
    assert lv % tm == 0 and n % tn == 0
    rope = rope_tabs is not None
    in_specs = [
        pl.BlockSpec((1, tm, d), lambda j, b, i: (b, i, 0)),
        pl.BlockSpec((1, d), lambda j, b, i: (0, 0)),
        pl.BlockSpec((d, tn), lambda j, b, i: (0, j)),
    ]
    args = [x3, g.reshape(1, d), w]
    if rope:
        in_specs += [pl.BlockSpec((tm, LANES), lambda j, b, i: (i, 0))] * 3
        args += list(rope_tabs)
    return pl.pallas_call(
        functools.partial(_norm_matmul_kernel, rope=rope, scale=scale),
        grid=(n // tn, bv, lv // tm),
        in_specs=in_specs,
        out_specs=pl.BlockSpec((1, tm, tn), lambda j, b, i: (b, i, j)),
        out_shape=jax.ShapeDtypeStruct((bv, lv, n), out_dtype),
        compiler_params=_params(("parallel", "parallel", "parallel")),
        name="norm_matmul",
    )(*args)


def _rope_tables(pos):
    half = DA_HEAD_DIM // 2
    inv_freq = ROPE_THETA ** (-jnp.arange(half, dtype=F32) * (2.0 / DA_HEAD_DIM))
    ang = pos.astype(F32)[:, None] * inv_freq[None, :]
    cos, sin = jnp.cos(ang), jnp.sin(ang)
    zero = jnp.zeros_like(sin)
    cos_t = jnp.concatenate([cos] * 4, axis=1)
    sa_t = jnp.concatenate([-sin, zero] * 2, axis=1)
    sb_t = jnp.concatenate([zero, sin] * 2, axis=1)
    return cos_t, sa_t, sb_t


def _lambda(lq1, lk1, lq2, lk2, lam_init):
    s1 = jnp.sum(lq1[...] * lk1[...], axis=-1, keepdims=True)
    s2 = jnp.sum(lq2[...] * lk2[...], axis=-1, keepdims=True)
    return jnp.exp(s1) - jnp.exp(s2) + lam_init


def _softmax_step(s, vb, m, l, acc):
    mn = jnp.maximum(m, jnp.max(s, axis=-1, keepdims=True))
    al = jnp.exp(m - mn)
    p = jnp.exp(s - mn)
    l = al * l + jnp.sum(p, axis=-1, keepdims=True)
    acc = al * acc + jnp.dot(p.astype(BF16), vb, preferred_element_type=F32)
    return mn, l, acc


def _attn_prompt_kernel(q_ref, k_ref, v_ref, lq1, lk1, lq2, lk2, sub_ref, o_ref, kb_scr, vb_scr,
                        *, seq, tile, lam_init):
    lam = _lambda(lq1, lk1, lq2, lk2, lam_init)
    sub = sub_ref[...] * (1.0 - lam_init)
    n_full = seq // tile
    tail = seq - n_full * tile
    pad_rows = kb_scr.shape[0] - seq
    kb_scr[0:seq, :] = k_ref[0].astype(BF16)
    vb_scr[0:seq, :] = v_ref[0].astype(BF16)
    if pad_rows:
        kb_scr[seq:seq + pad_rows, :] = jnp.zeros((pad_rows, LANES), BF16)
        vb_scr[seq:seq + pad_rows, :] = jnp.zeros((pad_rows, LANES), BF16)

    def q_tile(i, row0, tq):
        q = q_ref[0, pl.ds(row0, tq), :]
        lo = lax.broadcasted_iota(jnp.int32, (tq, LANES), 1) < DA_HEAD_DIM
        zero = jnp.zeros_like(q)
        qs = (jnp.where(lo, q, zero), jnp.where(lo, zero, q))

        def kv_step(j, st, masked):
            k0 = pl.multiple_of(j * tile, tile)
            kb = kb_scr[pl.ds(k0, tile), :]
            vb = vb_scr[pl.ds(k0, tile), :]
            out = []
            for c in range(2):
                s = lax.dot_general(qs[c], kb, NT_DIMS, preferred_element_type=F32)
                if masked:
                    keep = (lax.broadcasted_iota(jnp.int32, (tq, tile), 0)
                            >= lax.broadcasted_iota(jnp.int32, (tq, tile), 1))
                    s = jnp.where(keep, s, NEG)
                out.append(_softmax_step(s, vb, *st[c]))
            return tuple(out)

        init = tuple((jnp.full((tq, 1), NEG, F32), jnp.zeros((tq, 1), F32), jnp.zeros((tq, LANES), F32))
                     for _ in range(2))
        st = lax.fori_loop(0, i, lambda j, s: kv_step(j, s, False), init)
        st = kv_step(i, st, True)
        o = st[0][2] / st[0][1] - lam * (st[1][2] / st[1][1])
        ms = jnp.mean(o * o, axis=-1, keepdims=True)
        o_ref[0, pl.ds(row0, tq), :] = (o * lax.rsqrt(ms + SUBLN_EPS) * sub).astype(o_ref.dtype)

    def body(i, carry):
        q_tile(i, pl.multiple_of(i * tile, tile), tile)
        return carry

    lax.fori_loop(0, n_full, body, 0)
    if tail:
        q_tile(n_full, n_full * tile, tail)


def _attn_prompt(q, k, v, lam_vecs, subln, lam_init, *, tile):
    b, seq, _ = q.shape
    n_tiles = pl.cdiv(seq, tile)
    vec = pl.BlockSpec((1, DA_HEAD_DIM), lambda bi, h: (0, 0))
    blk = pl.BlockSpec((1, seq, LANES), lambda bi, h: (bi, 0, h))
    return pl.pallas_call(
        functools.partial(_attn_prompt_kernel, seq=seq, tile=tile, lam_init=lam_init),
        grid=(b, DA_HEADS),
        in_specs=[blk, blk, blk, vec, vec, vec, vec, pl.BlockSpec((1, LANES), lambda bi, h: (0, 0))],
        out_specs=blk,
        out_shape=jax.ShapeDtypeStruct((b, seq, DA_HEADS * DA_VALUE_DIM), BF16),
        scratch_shapes=[pltpu.VMEM((n_tiles * tile, LANES), BF16), pltpu.VMEM((n_tiles * tile, LANES), BF16)],
        compiler_params=_params(("parallel", "parallel")),
        name="attn_prompt",
    )(q, k, v, *lam_vecs, subln.reshape(1, LANES))


def _attn_sample_kernel(pt_ref, q_ref, k_ref, v_ref, kn_ref, vn_ref, lq1, lk1, lq2, lk2, sub_ref, o_ref,
                        qbd_scr, m_scr, l_scr, acc_scr, *, n_pages, n_new, lam_init):
    del pt_ref
    p = pl.program_id(1)
    rows = 2 * DA_HEADS * n_new
    width = q_ref.shape[2]

    @pl.when(p == 0)
    def _():
        q8 = q_ref[0]
        qt = jnp.concatenate([q8] * DA_HEADS, axis=0)
        r = lax.broadcasted_iota(jnp.int32, (rows, width), 0)
        c = lax.broadcasted_iota(jnp.int32, (rows, width), 1)
        qbd_scr[...] = jnp.where(c // DA_HEAD_DIM == r // n_new, qt, 0.0).astype(BF16)
        m_scr[...] = jnp.full(m_scr.shape, NEG, F32)
        l_scr[...] = jnp.zeros(l_scr.shape, F32)
        acc_scr[...] = jnp.zeros(acc_scr.shape, F32)

    def update(kb, vb, keep):
        s = lax.dot_general(qbd_scr[...], kb, NT_DIMS, preferred_element_type=F32)
        if keep is not None:
            s = jnp.where(keep, s, NEG)
        m, l, acc = _softmax_step(s, vb, m_scr[...], l_scr[...], acc_scr[...])
        m_scr[...] = m
        l_scr[...] = l
        acc_scr[...] = acc

    @pl.when(p < n_pages)
    def _():
        update(k_ref[...].astype(BF16), v_ref[...].astype(BF16), None)

    @pl.when(p == n_pages)
    def _():
        fill = jnp.zeros((PAGE_SIZE - kn_ref.shape[1], width), F32)
        kb = jnp.concatenate([kn_ref[0], fill], axis=0).astype(BF16)
        vb = jnp.concatenate([vn_ref[0], fill], axis=0).astype(BF16)
        r = lax.broadcasted_iota(jnp.int32, (rows, PAGE_SIZE), 0)
        c = lax.broadcasted_iota(jnp.int32, (rows, PAGE_SIZE), 1)
        update(kb, vb, c <= r % n_new)
        lam = _lambda(lq1, lk1, lq2, lk2, lam_init)
        sub = sub_ref[...] * (1.0 - lam_init)
        for h in range(DA_HEADS):
            r0 = h * 2 * n_new
            blk = acc_scr[r0:r0 + 2 * n_new, h * LANES:(h + 1) * LANES]
            y = blk / l_scr[r0:r0 + 2 * n_new, :]
            o = y - lam * pltpu.roll(y, n_new, 0)
            ms = jnp.mean(o * o, axis=-1, keepdims=True)
            o_ref[0, :, h * LANES:(h + 1) * LANES] = o * lax.rsqrt(ms + SUBLN_EPS) * sub


def _attn_sample(layer, q8, cache_k, cache_v, page_table, k_new8, v_new8, lam_vecs, subln, lam_init):
    nb, rows8, width = q8.shape
    n_new = rows8 // 2
    n_pages = page_table.shape[1]
    rows = 2 * DA_HEADS * n_new
    vec = pl.BlockSpec((1, DA_HEAD_DIM), lambda b, p, pt: (0, 0))
    seq_blk = pl.BlockSpec((1, rows8, width), lambda b, p, pt: (b, 0, 0))
    page_blk = pl.BlockSpec((None, None, PAGE_SIZE, width),
                            lambda b, p, pt: (layer, pt[b, jnp.minimum(p, n_pages - 1)], 0, 0))
    grid_spec = pltpu.PrefetchScalarGridSpec(
        num_scalar_prefetch=1,
        grid=(nb, n_pages + 1),
        in_specs=[seq_blk, page_blk, page_blk, seq_blk, seq_blk, vec, vec, vec, vec,
                  pl.BlockSpec((1, LANES), lambda b, p, pt: (0, 0))],
        out_specs=seq_blk,
        scratch_shapes=[pltpu.VMEM((rows, width), BF16), pltpu.VMEM((rows, 1), F32),
                        pltpu.VMEM((rows, 1), F32), pltpu.VMEM((rows, width), F32)],
    )
    return pl.pallas_call(
        functools.partial(_attn_sample_kernel, n_pages=n_pages, n_new=n_new, lam_init=lam_init),
        grid_spec=grid_spec,
        out_shape=jax.ShapeDtypeStruct((nb, rows8, width), F32),
        compiler_params=_params(("parallel", "arbitrary")),
        name="attn_sample",
    )(page_table, q8, cache_k, cache_v, k_new8, v_new8, *lam_vecs, subln.reshape(1, LANES))


def _split_bf16(x):
    hi = x.astype(BF16)
    lo = (x - hi.astype(F32)).astype(BF16)
    return hi, lo


def _head_sums(x, seg):
    outs = []
    for p in range(x.shape[1] // LANES):
        hi, lo = _split_bf16(x[:, p * LANES:(p + 1) * LANES])
        outs.append(jnp.dot(hi, seg, preferred_element_type=F32) + jnp.dot(lo, seg, preferred_element_type=F32))
    return jnp.concatenate(outs, axis=1)


def _rwkv_kernel(*refs, chunk, seq, layer0):
    if layer0:
        (zr_ref, sh0_ref, s0_ref, mu_ref, w0_ref, w2_ref, a0_ref, a2_ref, g2_ref,
         kk_ref, ka_ref, rk_ref, lnw_ref, lnb_ref,
         y_ref, sout_ref, shout_ref, vf_out_ref, sbd_scr, buf_scr) = refs
    else:
        (zr_ref, sh0_ref, s0_ref, vf_ref, mu_ref, w0_ref, w2_ref, a0_ref, a2_ref, g2_ref,
         v0_ref, v1_ref, v2_ref, kk_ref, ka_ref, rk_ref, lnw_ref, lnb_ref,
         y_ref, sout_ref, shout_ref, sbd_scr, buf_scr) = refs
    C = chunk
    W = RW_WIDTH
    c_idx = pl.program_id(1)
    n_chunks = pl.num_programs(1)
    zero64 = jnp.zeros((RW_HEAD, RW_HEAD), F32)

    @pl.when(c_idx == 0)
    def _():
        for p in range(RW_PAIRS):
            top = jnp.concatenate([s0_ref[0, 2 * p], zero64], axis=1)
            bot = jnp.concatenate([zero64, s0_ref[0, 2 * p + 1]], axis=1)
            sbd_scr[p] = jnp.concatenate([top, bot], axis=0)
        buf_scr[7:8, :] = sh0_ref[0]

    rows = c_idx * C + lax.broadcasted_iota(jnp.int32, (C, 1), 0)
    valid = rows < seq
    zr = jnp.where(valid, zr_ref[0], 0.0)
    buf_scr[8:8 + C, :] = zr
    prev = buf_scr[7:7 + C, :]
    buf_scr[7:8, :] = zr[C - 1:C, :]
    m = jnp.where(valid, zr + (prev - zr) * mu_ref[...], 0.0)

    r = m[:, 0:W]
    k = m[:, W:2 * W]
    v = m[:, 2 * W:3 * W]
    x2 = m[:, 3 * W:3 * W + LANES]
    gd = m[:, 3 * W + LANES:3 * W + 2 * LANES]

    wpre = w0_ref[...] + _bdot(jnp.tanh(x2), w2_ref[...])
    nw = -wpre
    softplus = jnp.maximum(nw, 0.0) + jnp.log(1.0 + jnp.exp(-jnp.abs(nw)))
    lw = jnp.where(valid, -jnp.exp(-softplus - 0.5), 0.0)
    a = _sigmoid(a0_ref[...] + _bdot(x2, a2_ref[...]))
    g = _bdot(_sigmoid(gd), g2_ref[...])
    if layer0:
        vf_out_ref[0] = v
    else:
        vf = jnp.where(valid, vf_ref[0], 0.0)
        v = v + (vf - v) * _sigmoid(v0_ref[...] + _bdot(_bdot(v, v1_ref[...]), v2_ref[...]))

    lane = lax.broadcasted_iota(jnp.int32, (LANES, LANES), 1)
    sub = lax.broadcasted_iota(jnp.int32, (LANES, LANES), 0)
    seg = (lane // RW_HEAD == sub // RW_HEAD).astype(BF16)

    kkv = k * kk_ref[...]
    kkn = kkv * lax.rsqrt(jnp.maximum(_head_sums(kkv * kkv, seg), 1e-24))
    k2 = k * (1.0 + (a - 1.0) * ka_ref[...])
    aa = -kkn
    bb = kkn * a
    bonus = _head_sums(r * k2 * rk_ref[...], seg)

    tri = (lax.broadcasted_iota(jnp.int32, (C, C), 0) >= lax.broadcasted_iota(jnp.int32, (C, C), 1)).astype(BF16)
    lw_hi, lw_lo = _split_bf16(lw)
    cl = jnp.dot(tri, lw_hi, preferred_element_type=F32) + jnp.dot(tri, lw_lo, preferred_element_type=F32)
    cl_end = cl[C - 1:C, :]
    p_in = jnp.exp(cl)
    p_ex = jnp.exp(cl - lw)
    p_inv = jnp.exp(-cl)
    p_end = jnp.exp(cl_end - cl)
    p_tot = jnp.exp(cl_end)

    a_t = aa * p_ex
    r_t = r * p_in
    b_t = bb * p_inv
    k_t = k2 * p_inv
    b_h = bb * p_end
    k_h = k2 * p_end

    lane_c = lax.broadcasted_iota(jnp.int32, (C, LANES), 1)
    m1 = (lane_c < RW_HEAD).astype(F32)
    m2 = 1.0 - m1

    def stack(x, p):
        xp = x[:, p * LANES:(p + 1) * LANES]
        return jnp.concatenate([xp * m1, xp * m2], axis=0)

    rr = lax.broadcasted_iota(jnp.int32, (2 * C, 2 * C), 0)
    cc = lax.broadcasted_iota(jnp.int32, (2 * C, 2 * C), 1)
    same = (rr // C) == (cc // C)
    strict = jnp.logical_and(same, (rr % C) > (cc % C))
    incl = jnp.logical_and(same, (rr % C) >= (cc % C))
    base = 8
    eye = (rr == cc).astype(F32)
    base_mask = jnp.logical_and(rr // base == cc // base, rr > cc)
    merge_masks = []
    kb = base
    while kb < C:
        merge_masks.append(jnp.logical_and(rr // (2 * kb) == cc // (2 * kb), rr // kb == cc // kb + 1))
        kb *= 2

    ys = []
    for p in range(RW_PAIRS):
        s_bd = sbd_scr[p]
        ar = jnp.concatenate([stack(a_t, p), stack(r_t, p)], axis=0).astype(BF16)
        bk = jnp.concatenate([stack(b_t, p), stack(k_t, p)], axis=0).astype(BF16)
        vm = stack(v, p)
        mat = lax.dot_general(ar, bk, NT_DIMS, preferred_element_type=F32)
        l_ab = jnp.where(strict, mat[:2 * C, :2 * C], 0.0)
        a_ak = jnp.where(strict, mat[:2 * C, 2 * C:], 0.0)
        a_rb = jnp.where(incl, mat[2 * C:, :2 * C], 0.0)
        a_rk = jnp.where(incl, mat[2 * C:, 2 * C:], 0.0)
        gs = lax.dot_general(ar, s_bd.astype(BF16), NT_DIMS, preferred_element_type=F32)
        l8 = jnp.where(base_mask, l_ab, 0.0)
        inv = eye + l8
        lp = l8
        for _ in range(2):
            lp = _bdot(lp, lp)
            inv = inv + _bdot(lp, inv)
        for em in merge_masks:
            inv = inv + _bdot(inv, _bdot(jnp.where(em, l_ab, 0.0), inv))
        x = _bdot(inv, gs[:2 * C] + _bdot(a_ak, vm))
        uv = jnp.concatenate([x, vm], axis=0)
        ym = gs[2 * C:] + _bdot(jnp.concatenate([a_rb, a_rk], axis=1), uv)
        ys.append(ym[:C] + ym[C:])
        bkh = jnp.concatenate([stack(b_h, p), stack(k_h, p)], axis=0)
        sbd_scr[p] = s_bd * p_tot[:, p * LANES:(p + 1) * LANES] + _bdot(uv.T, bkh)
    y = jnp.concatenate(ys, axis=1)

    inv_n = 1.0 / RW_HEAD
    mean = _head_sums(y, seg) * inv_n
    d = y - mean
    var = _head_sums(d * d, seg) * inv_n
    yn = d * lax.rsqrt(var + LNX_EPS) * lnw_ref[...] + lnb_ref[...]
    y_ref[0] = ((yn + bonus * v) * g).astype(y_ref.dtype)

    @pl.when(c_idx == n_chunks - 1)
    def _():
        last = (seq - 1) % C
        shout_ref[0] = zr[last:last + 1, :]
        for p in range(RW_PAIRS):
            s_bd = sbd_scr[p]
            sout_ref[0, 2 * p] = s_bd[0:RW_HEAD, 0:RW_HEAD]
            sout_ref[0, 2 * p + 1] = s_bd[RW_HEAD:, RW_HEAD:]


def _rwkv(zr, shift0, wkv0, v_first, prm, *, chunk, seq, layer0):
    b, lpad, cols = zr.shape
    n_chunks = pl.cdiv(lpad, chunk)
    W = RW_WIDTH
    row = lambda n: pl.BlockSpec((1, n), lambda bi, c: (0, 0))
    full = lambda s: pl.BlockSpec(s, lambda bi, c: (0, 0))
    tok = lambda n: pl.BlockSpec((1, chunk, n), lambda bi, c: (bi, c, 0))
    in_specs = [tok(cols), pl.BlockSpec((1, 1, cols), lambda bi, c: (bi, 0, 0)),
                pl.BlockSpec((1, RW_HEADS, RW_HEAD, RW_HEAD), lambda bi, c: (bi, 0, 0, 0))]
    args = [zr, shift0.reshape(b, 1, cols), wkv0]
    if not layer0:
        in_specs.append(tok(W))
        args.append(v_first)
    in_specs += [row(cols), row(W), full((LANES, W)), row(W), full((LANES, W)), full((GATE_LORA, W))]
    args += [prm['mu'], prm['w0'], prm['w2'], prm['a0'], prm['a2'], prm['g2']]
    if not layer0:
        in_specs += [row(W), full((W, LANES)), full((LANES, W))]
        args += [prm['v0'], prm['v1'], prm['v2']]
    in_specs += [row(W)] * 5
    args += [prm['k_k'], prm['k_a'], prm['r_k'], prm['ln_w'], prm['ln_b']]
    out_shape = [jax.ShapeDtypeStruct((b, lpad, W), BF16),
                 jax.ShapeDtypeStruct((b, RW_HEADS, RW_HEAD, RW_HEAD), F32),
                 jax.ShapeDtypeStruct((b, 1, cols), F32)]
    out_specs = [tok(W), pl.BlockSpec((1, RW_HEADS, RW_HEAD, RW_HEAD), lambda bi, c: (bi, 0, 0, 0)),
                 pl.BlockSpec((1, 1, cols), lambda bi, c: (bi, 0, 0))]
    if layer0:
        out_shape.append(jax.ShapeDtypeStruct((b, lpad, W), F32))
        out_specs.append(tok(W))
    outs = pl.pallas_call(
        functools.partial(_rwkv_kernel, chunk=chunk, seq=seq, layer0=layer0),
        grid=(b, n_chunks),
        in_specs=in_specs,
        out_specs=out_specs,
        out_shape=out_shape,
        scratch_shapes=[pltpu.VMEM((RW_PAIRS, LANES, LANES), F32), pltpu.VMEM((chunk + 8, cols), F32)],
        compiler_params=_params(("parallel", "arbitrary")),
        name="rwkv",
    )(*args)
    if layer0:
        y, s_out, sh_out, vf = outs
    else:
        (y, s_out, sh_out), vf = outs, v_first
    return y, s_out, sh_out.reshape(b, cols), vf


def _rms(x, g):
    ms = jnp.mean(x * x, axis=-1, keepdims=True)
    return x * lax.rsqrt(ms + 1e-6) * g


def _merge_kernel(att_ref, rw_ref, ga_ref, gb_ref, x_ref, wa_ref, wb_ref, wo_ref, n2_ref, o_ref):
    ya = _bdot(att_ref[...], wa_ref[...])
    yb = _bdot(rw_ref[...], wb_ref[...])
    merged = _sigmoid(ga_ref[...].astype(F32)) * ya + _sigmoid(gb_ref[...].astype(F32)) * yb
    mix = _bdot(merged, wo_ref[...])
    o_ref[...] = x_ref[...] + _rms(mix, n2_ref[...])


def _merge(att, rw, gates, x, wa, wb, wo, n2, *, tm):
    t, d = x.shape
    assert t % tm == 0
    tokb = lambda j: pl.BlockSpec((tm, d), lambda i: (i, j))
    wspec = pl.BlockSpec((d, d), lambda i: (0, 0))
    return pl.pallas_call(
        _merge_kernel,
        grid=(t // tm,),
        in_specs=[tokb(0), tokb(0), tokb(0), tokb(1), tokb(0), wspec, wspec, wspec,
                  pl.BlockSpec((1, d), lambda i: (0, 0))],
        out_specs=tokb(0),
        out_shape=jax.ShapeDtypeStruct((t, d), F32),
        compiler_params=_params(("parallel",)),
        name="merge",
    )(att, rw, gates, gates, x, wa, wb, wo, n2.reshape(1, d))


def _ffn_kernel(x_ref, n3_ref, wu_ref, wd_ref, n4_ref, o_ref, *, n_split):
    x = x_ref[...]
    h = _rms(x, n3_ref[...]).astype(BF16)
    dff = wu_ref.shape[1]
    step = dff // n_split
    f = jnp.zeros(x.shape, F32)
    for s in range(n_split):
        u = jnp.dot(h, wu_ref[:, s * step:(s + 1) * step], preferred_element_type=F32)
        u = jnp.square(jnp.maximum(u, 0.0))
        f = f + jnp.dot(u.astype(BF16), wd_ref[s * step:(s + 1) * step, :], preferred_element_type=F32)
    o_ref[...] = x + _rms(f, n4_ref[...])


def _ffn(x, n3, wu, wd, n4, *, tm):
    t, d = x.shape
    dff = wu.shape[1]
    assert t % tm == 0
    tokb = pl.BlockSpec((tm, d), lambda i: (i, 0))
    vec = pl.BlockSpec((1, d), lambda i: (0, 0))
    return pl.pallas_call(
        functools.partial(_ffn_kernel, n_split=4),
        grid=(t // tm,),
        in_specs=[tokb, vec, pl.BlockSpec((d, dff), lambda i: (0, 0)), pl.BlockSpec((dff, d), lambda i: (0, 0)), vec],
        out_specs=tokb,
        out_shape=jax.ShapeDtypeStruct((t, d), F32),
        compiler_params=_params(("parallel",)),
        name="ffn",
    )(x, n3.reshape(1, d), wu, wd, n4.reshape(1, d))


def _pick_tile(n, candidates):
    for c in candidates:
        if n % c == 0:
            return c
    return n


def kernel(x_prompt, x_sample, cache_k, cache_v, state_wkv, state_shift, page_table, meta_tokens, norm_pre_mix, norm_post_mix, norm_pre_ffn, norm_post_ffn, w_in, lambda_q1, lambda_k1, lambda_q2, lambda_k2, da_subln, rw_mu, rw_w0, rw_w2, rw_a0, rw_a2, rw_g2, rw_v0, rw_v1, rw_v2, rw_k_k, rw_k_a, rw_r_k, rw_ln_w, rw_ln_b, w_branch_a, w_branch_b, w_out, w_up, w_down):
    depth = w_in.shape[0]
    b, seq_in, d = x_prompt.shape
    db, ds, _ = x_sample.shape
    seq = seq_in + N_META
    n_pages = page_table.shape[1]
    n_past = n_pages * PAGE_SIZE
    qk_w = 2 * DA_HEADS * DA_HEAD_DIM
    da_w = DA_HEADS * DA_VALUE_DIM

    xp = jnp.concatenate([jnp.broadcast_to(meta_tokens[None], (b, N_META, d)), x_prompt], axis=1)
    xs = x_sample.reshape(1, db * ds, d)
    tabs_p = _rope_tables(jnp.arange(seq))
    tabs_s = _rope_tables(n_past + jnp.arange(db * ds) % ds)
    tm_p = _pick_tile(seq, (688, 344, 72, 48, 16))
    tm_s = _pick_tile(db * ds, (512, 256, 128, 64, 32, 16, 8))
    tm_flat_p = _pick_tile(b * seq, (384, 256, 128, 64, 48, 32, 16, 8))
    attn_tile = 256 if seq >= 512 else 64
    chunk_p = 64
    chunk_s = 8

    cache_k4 = cache_k.reshape(depth, cache_k.shape[1], PAGE_SIZE, qk_w)
    cache_v4 = cache_v.reshape(depth, cache_v.shape[1], PAGE_SIZE, da_w)
    w_in_b = w_in.astype(BF16)
    zeros_lora = jnp.zeros((DECAY_LORA, RW_WIDTH), BF16)
    wkv0_p = jnp.zeros((b, RW_HEADS, RW_HEAD, RW_HEAD), F32)
    shift0_p = jnp.zeros((b, RW_COLS), F32)

    vf_p = vf_s = None
    outs = [[] for _ in range(8)]
    for l in range(depth):
        o = 0
        wq = w_in_b[l, :, o:o + qk_w]; o += qk_w
        wk = w_in_b[l, :, o:o + qk_w]; o += qk_w
        wv = w_in_b[l, :, o:o + da_w]; o += da_w
        wz = w_in_b[l, :, o:o + RW_COLS]; o += RW_COLS
        wg = w_in_b[l, :, o:o + 2 * d]
        row = lambda t: t.reshape(1, -1).astype(F32)
        prm = {'mu': row(rw_mu[l]), 'w0': row(rw_w0[l]), 'a0': row(rw_a0[l]),
               'w2': jnp.concatenate([rw_w2[l].astype(BF16), zeros_lora], axis=0),
               'a2': jnp.concatenate([zeros_lora, rw_a2[l].astype(BF16)], axis=0),
               'g2': rw_g2[l].astype(BF16),
               'k_k': row(rw_k_k[l]), 'k_a': row(rw_k_a[l]), 'r_k': row(rw_r_k[l]),
               'ln_w': row(rw_ln_w[l]), 'ln_b': row(rw_ln_b[l])}
        if l > 0:
            prm['v0'] = row(rw_v0[l - 1])
            prm['v1'] = jnp.pad(rw_v1[l - 1].astype(BF16), ((0, 0), (0, LANES - MV_LORA)))
            prm['v2'] = jnp.pad(rw_v2[l - 1].astype(BF16), ((0, LANES - MV_LORA), (0, 0)))
        lam_init = 0.8 - 0.6 * math.exp(-0.3 * l)
        lam_vecs = [t[l].reshape(1, DA_HEAD_DIM).astype(F32) for t in (lambda_q1, lambda_k1, lambda_q2, lambda_k2)]
        wa, wb, wo = w_branch_a[l].astype(BF16), w_branch_b[l].astype(BF16), w_out[l].astype(BF16)
        wu, wd = w_up[l].astype(BF16), w_down[l].astype(BF16)
        n1 = norm_pre_mix[l]
        scale = DA_HEAD_DIM ** -0.5

        def project(x3, tm, tabs):
            q = _norm_matmul(x3, n1, wq, tm=tm, tn=qk_w, out_dtype=F32, rope_tabs=tabs, scale=scale)
            k = _norm_matmul(x3, n1, wk, tm=tm, tn=qk_w, out_dtype=F32, rope_tabs=tabs)
            v = _norm_matmul(x3, n1, wv, tm=tm, tn=da_w, out_dtype=F32)
            zr = _norm_matmul(x3, n1, wz, tm=tm, tn=RW_COLS // 2, out_dtype=F32)
            gt = _norm_matmul(x3, n1, wg, tm=tm, tn=d, out_dtype=BF16)
            return q, k, v, zr, gt

        q, k, v, zr, gt = project(xp, tm_p, tabs_p)
        att = _attn_prompt(q.astype(BF16), k, v, lam_vecs, da_subln[l], lam_init, tile=attn_tile)
        rw, wkv_p, sh_p, vf_p = _rwkv(zr, shift0_p, wkv0_p, vf_p, prm, chunk=chunk_p, seq=seq, layer0=(l == 0))
        flat = lambda t: t.reshape(b * seq, -1)
        x2 = _merge(flat(att), flat(rw), flat(gt), flat(xp), wa, wb, wo, norm_post_mix[l], tm=tm_flat_p)
        x2 = _ffn(x2, norm_pre_ffn[l], wu, wd, norm_post_ffn[l], tm=tm_flat_p)
        xp = x2.reshape(b, seq, d)
        outs[0].append(k.reshape(b, seq, 2 * DA_HEADS, DA_HEAD_DIM))
        outs[1].append(v.reshape(b, seq, DA_HEADS, DA_VALUE_DIM))
        outs[2].append(wkv_p)
        outs[3].append(sh_p)

        q, k, v, zr, gt = project(xs, tm_s, tabs_s)
        per_seq = lambda t: t.reshape(db, ds, -1)
        pad8 = lambda t: jnp.pad(per_seq(t), ((0, 0), (0, 2 * ds - ds), (0, 0)))
        q8 = jnp.concatenate([per_seq(q), per_seq(q)], axis=1)
        att8 = _attn_sample(l, q8, cache_k4, cache_v4, page_table, pad8(k), pad8(v), lam_vecs, da_subln[l], lam_init)
        att = att8[:, :ds].reshape(db * ds, da_w)
        if l > 0 and vf_s.shape[1] != chunk_s:
            vf_s = jnp.pad(vf_s, ((0, 0), (0, chunk_s - vf_s.shape[1]), (0, 0)))
        zr8 = jnp.pad(per_seq(zr), ((0, 0), (0, chunk_s - ds), (0, 0)))
        rw8, wkv_s, sh_s, vf_s = _rwkv(zr8, state_shift[l], state_wkv[l], vf_s, prm, chunk=chunk_s, seq=ds,
                                        layer0=(l == 0))
        rw = rw8[:, :ds].reshape(db * ds, RW_WIDTH)
        x2 = _merge(att, rw, gt.reshape(db * ds, -1), xs.reshape(db * ds, d), wa, wb, wo, norm_post_mix[l], tm=tm_s)
        x2 = _ffn(x2, norm_pre_ffn[l], wu, wd, norm_post_ffn[l], tm=tm_s)
        xs = x2.reshape(1, db * ds, d)
        outs[4].append(k.reshape(db, ds, 2 * DA_HEADS, DA_HEAD_DIM))
        outs[5].append(v.reshape(db, ds, DA_HEADS, DA_VALUE_DIM))
        outs[6].append(wkv_s)
        outs[7].append(sh_s)

    y_prompt = xp[:, N_META:]
    y_sample = xs.reshape(db, ds, d)
    return (y_prompt, y_sample) + tuple(jnp.stack(o) for o in outs)
```

```python
import functools
import math

import jax
import jax.numpy as jnp
from jax import lax
from jax.experimental import pallas as pl
from jax.experimental.pallas import tpu as pltpu

F32 = jnp.float32
BF16 = jnp.bfloat16

D_MODEL = 1024
N_META = 16
DA_HEADS = 8
DA_MAPS = 2 * DA_HEADS
DA_HEAD_DIM = 64
DA_VALUE_DIM = 128
ROPE_THETA = 10000.0
SUBLN_EPS = 1e-5
RW_HEAD = 64
RW_HEADS = 16
RW_PAIRS = RW_HEADS // 2
RW_WIDTH = 1024
DECAY_LORA = 64
AAA_LORA = 64
GATE_LORA = 128
MV_LORA = 32
RW_COLS = 3 * RW_WIDTH + DECAY_LORA + AAA_LORA + GATE_LORA
LNX_EPS = 64e-5
PAGE_SIZE = 128
LANES = 128
SUBLANES = 8
NEG = -1e30
LOG2E = 1.4426950408889634
VMEM_LIMIT = 52 * 1024 * 1024

NT_DIMS = (((1,), (1,)), ((), ()))


def _bdot(a, b):
    return jnp.dot(a.astype(BF16), b.astype(BF16), preferred_element_type=F32)


def _bmm(a, b):
    return jnp.einsum('pij,pjk->pik', a.astype(BF16), b.astype(BF16), preferred_element_type=F32)


def _bmm_nt(a, b):
    return jnp.einsum('pik,pjk->pij', a.astype(BF16), b.astype(BF16), preferred_element_type=F32)


def _sigmoid(x):
    return 1.0 / (1.0 + jnp.exp(-x))


def _params(sem):
    return pltpu.CompilerParams(dimension_semantics=sem, vmem_limit_bytes=VMEM_LIMIT)


def _norm_matmul_kernel(x_ref, g_ref, w_ref, *rest, rope, scale):
    if rope:
        cos_ref, sa_ref, sb_ref, o_ref = rest
    else:
        (o_ref,) = rest
    x = x_ref[0]
    ms = jnp.mean(x * x, axis=-1, keepdims=True)
    h = (x * lax.rsqrt(ms + 1e-6) * g_ref[...]).astype(BF16)
    acc = jnp.dot(h, w_ref[...], preferred_element_type=F32)
    if rope:
        cos, sa, sb = cos_ref[...], sa_ref[...], sb_ref[...]
        for c in range(acc.shape[1] // LANES):
            blk = acc[:, c * LANES:(c + 1) * LANES]
            out = blk * cos + pltpu.roll(blk, 96, 1) * sa + pltpu.roll(blk, 32, 1) * sb
            if scale != 1.0:
                out = out * scale
            o_ref[0, :, c * LANES:(c + 1) * LANES] = out.astype(o_ref.dtype)
    else:
        o_ref[0] = acc.astype(o_ref.dtype)


def _norm_matmul(x3, g, w, *, tm, tn, out_dtype, rope_tabs=None, scale=1.0):
    bv, lv, d = x3.shape
    n = w.shape[1]
    assert lv % tm == 0 and n % tn == 0
    rope = rope_tabs is not None
    in_specs = [
        pl.BlockSpec((1, tm, d), lambda j, b, i: (b, i, 0)),
        pl.BlockSpec((1, d), lambda j, b, i: (0, 0)),
        pl.BlockSpec((d, tn), lambda j, b, i: (0, j)),
    ]
    args = [x3, g.reshape(1, d), w]
    if rope:
        in_specs += [pl.BlockSpec((tm, LANES), lambda j, b, i: (i, 0))] * 3
        args += list(rope_tabs)
    return pl.pallas_call(
        functools.partial(_norm_matmul_kernel, rope=rope, scale=scale),
        grid=(n // tn, bv, lv // tm),
        in_specs=in_specs,
        out_specs=pl.BlockSpec((1, tm, tn), lambda j, b, i: (b, i, j)),
        out_shape=jax.ShapeDtypeStruct((bv, lv, n), out_dtype),
        compiler_params=_params(("parallel", "parallel", "parallel")),
        name="norm_matmul",
    )(*args)


def _rope_tables(pos):
    half = DA_HEAD_DIM // 2
    inv_freq = ROPE_THETA ** (-jnp.arange(half, dtype=F32) * (2.0 / DA_HEAD_DIM))
    ang = pos.astype(F32)[:, None] * inv_freq[None, :]
    cos, sin = jnp.cos(ang), jnp.sin(ang)
    zero = jnp.zeros_like(sin)
    cos_t = jnp.concatenate([cos] * 4, axis=1)
    sa_t = jnp.concatenate([-sin, zero] * 2, axis=1)
    sb_t = jnp.concatenate([zero, sin] * 2, axis=1)
    return cos_t, sa_t, sb_t


def _lambda(lq1, lk1, lq2, lk2, lam_init):
    s1 = jnp.sum(lq1[...] * lk1[...], axis=-1, keepdims=True)
    s2 = jnp.sum(lq2[...] * lk2[...], axis=-1, keepdims=True)
    return jnp.exp(s1) - jnp.exp(s2) + lam_init


def _attn_prompt_kernel(q_ref, k_ref, v_ref, lq1, lk1, lq2, lk2, sub_ref, o_ref, kb_scr, vb_scr, s_scr, p_scr,
                        *, seq, tq, lam_init):
    lam = _lambda(lq1, lk1, lq2, lk2, lam_init)
    sub = sub_ref[...] * (1.0 - lam_init)
    n_tiles = pl.cdiv(seq, tq)
    pad_rows = kb_scr.shape[0] - seq
    kb_scr[0:seq, :] = k_ref[0].astype(BF16)
    vb_scr[0:seq, :] = v_ref[0].astype(BF16)
    if pad_rows:
        kb_scr[seq:seq + pad_rows, :] = jnp.zeros((pad_rows, LANES), BF16)
        vb_scr[seq:seq + pad_rows, :] = jnp.zeros((pad_rows, LANES), BF16)

    for i in range(n_tiles):
        row0 = i * tq
        rows = min(tq, seq - row0)
        nk = row0 + tq
        q = q_ref[0, row0:row0 + rows, :]
        lo = lax.broadcasted_iota(jnp.int32, (rows, LANES), 1) < DA_HEAD_DIM
        zero = jnp.zeros_like(q)
        q2 = jnp.concatenate([jnp.where(lo, q, zero), jnp.where(lo, zero, q)], axis=0)
        s = lax.dot_general(q2, kb_scr[0:nk, :], NT_DIMS, preferred_element_type=F32)
        qpos = lax.broadcasted_iota(jnp.int32, (2 * rows, tq), 0) % rows
        kpos = lax.broadcasted_iota(jnp.int32, (2 * rows, tq), 1)
        n_blk = nk // LANES
        blocks = [s[:, b * LANES:(b + 1) * LANES] for b in range(n_blk)]
        for b in range(row0 // LANES, n_blk):
            off = b * LANES - row0
            blocks[b] = jnp.where(kpos[:, off:off + LANES] <= qpos[:, off:off + LANES], blocks[b], NEG)
        m_acc = functools.reduce(jnp.maximum, blocks)
        for b in range(n_blk):
            s_scr[0:2 * rows, b * LANES:(b + 1) * LANES] = blocks[b]
        m = jnp.max(m_acc, axis=-1, keepdims=True)
        l_acc = jnp.zeros((2 * rows, LANES), F32)
        for b in range(n_blk):
            p = jnp.exp2(s_scr[0:2 * rows, b * LANES:(b + 1) * LANES] - m)
            l_acc = l_acc + p
            p_scr[0:2 * rows, b * LANES:(b + 1) * LANES] = p.astype(BF16)
        l = jnp.sum(l_acc, axis=-1, keepdims=True)
        acc = jnp.dot(p_scr[0:2 * rows, 0:nk], vb_scr[0:nk, :], preferred_element_type=F32) / l
        o = acc[0:rows] - lam * acc[rows:2 * rows]
        ms = jnp.mean(o * o, axis=-1, keepdims=True)
        o_ref[0, row0:row0 + rows, :] = (o * lax.rsqrt(ms + SUBLN_EPS) * sub).astype(o_ref.dtype)


def _attn_prompt(q, k, v, lam_vecs, subln, lam_init, *, tq):
    b, seq, _ = q.shape
    npad = pl.cdiv(seq, tq) * tq
    vec = pl.BlockSpec((1, DA_HEAD_DIM), lambda bi, h: (0, 0))
    blk = pl.BlockSpec((1, seq, LANES), lambda bi, h: (bi, 0, h))
    return pl.pallas_call(
        functools.partial(_attn_prompt_kernel, seq=seq, tq=tq, lam_init=lam_init),
        grid=(b, DA_HEADS),
        in_specs=[blk, blk, blk, vec, vec, vec, vec, pl.BlockSpec((1, LANES), lambda bi, h: (0, 0))],
        out_specs=blk,
        out_shape=jax.ShapeDtypeStruct((b, seq, DA_HEADS * DA_VALUE_DIM), BF16),
        scratch_shapes=[pltpu.VMEM((npad, LANES), BF16), pltpu.VMEM((npad, LANES), BF16),
                        pltpu.VMEM((2 * tq, npad), F32), pltpu.VMEM((2 * tq, npad), BF16)],
        compiler_params=_params(("parallel", "parallel")),
        name="attn_prompt",
    )(q, k, v, *lam_vecs, subln.reshape(1, LANES))


def _attn_sample_kernel(pt_ref, q_ref, *refs, n_steps, pp, n_new, lam_init):
    del pt_ref
    k_refs, v_refs = refs[:pp], refs[pp:2 * pp]
    kn_ref, vn_ref, lq1, lk1, lq2, lk2, sub_ref, o_ref, m_scr, l_scr, acc_scr = refs[2 * pp:]
    p = pl.program_id(1)
    rpm = SUBLANES
    rows = DA_MAPS * rpm

    @pl.when(p == 0)
    def _():
        m_scr[...] = jnp.full(m_scr.shape, NEG, F32)
        l_scr[...] = jnp.zeros(l_scr.shape, F32)
        acc_scr[...] = jnp.zeros(acc_scr.shape, F32)

    def update(scores, get_v, keep):
        s = jnp.concatenate([scores(q_ref[0, c].astype(BF16), c) for c in range(DA_MAPS)], axis=0)
        if keep is not None:
            s = jnp.where(keep, s, NEG)
        m_old = m_scr[...]
        mn = jnp.maximum(m_old, jnp.max(s, axis=-1, keepdims=True))
        al = jnp.exp2(m_old - mn)
        pr = jnp.exp2(s - mn)
        l_scr[...] = al * l_scr[...] + jnp.sum(pr, axis=-1, keepdims=True)
        m_scr[...] = mn
        pb = pr.astype(BF16)
        pv = jnp.concatenate(
            [jnp.dot(pb[2 * rpm * h:2 * rpm * (h + 1)], get_v(h), preferred_element_type=F32)
             for h in range(DA_HEADS)], axis=0)
        acc_scr[...] = al * acc_scr[...] + pv

    @pl.when(p < n_steps)
    def _():
        update(lambda qc, c: jnp.concatenate(
                   [jnp.dot(qc, k_ref[c].astype(BF16), preferred_element_type=F32) for k_ref in k_refs], axis=1),
               lambda h: jnp.concatenate(
                   [v_ref[pl.ds(h, PAGE_SIZE, stride=DA_HEADS), :].astype(BF16) for v_ref in v_refs], axis=0),
               None)

    @pl.when(p == n_steps)
    def _():
        n_fill = PAGE_SIZE - kn_ref.shape[1]
        kfill = jnp.zeros((n_fill, DA_HEAD_DIM), F32)
        vfill = jnp.zeros((n_fill, DA_VALUE_DIM), F32)
        r = lax.broadcasted_iota(jnp.int32, (rows, PAGE_SIZE), 0) % rpm
        c = lax.broadcasted_iota(jnp.int32, (rows, PAGE_SIZE), 1)
        keep = jnp.logical_and(c <= r, c < n_new)
        update(lambda qc, ci: lax.dot_general(
                   qc, jnp.concatenate([kn_ref[0, :, ci, :], kfill], axis=0).astype(BF16), NT_DIMS,
                   preferred_element_type=F32),
               lambda h: jnp.concatenate([vn_ref[0, :, h, :], vfill], axis=0).astype(BF16), keep)
        lam = _lambda(lq1, lk1, lq2, lk2, lam_init)
        sub = sub_ref[...] * (1.0 - lam_init)
        y = acc_scr[...] / l_scr[...]
        for h in range(DA_HEADS):
            r0 = 2 * rpm * h
            o = y[r0:r0 + rpm] - lam * y[r0 + rpm:r0 + 2 * rpm]
            ms = jnp.mean(o * o, axis=-1, keepdims=True)
            o_ref[0, :, h * LANES:(h + 1) * LANES] = o * lax.rsqrt(ms + SUBLN_EPS) * sub


def _attn_sample(layer, q, cache_kt, cache_v2, page_table, k_new, v_new, lam_vecs, subln, lam_init, n_new):
    nb = q.shape[0]
    n_pages = page_table.shape[1]
    pp = _pick_tile(n_pages, (4, 2, 1))
    n_steps = n_pages // pp
    rows = DA_MAPS * SUBLANES
    vec = pl.BlockSpec((1, DA_HEAD_DIM), lambda b, p, pt: (0, 0))
    seq4 = lambda s: pl.BlockSpec((1,) + s, lambda b, p, pt: (b, 0, 0, 0))

    def page_of(j):
        return lambda b, p, pt: pt[b, jnp.minimum(p, n_steps - 1) * pp + j]

    k_specs = [pl.BlockSpec((None, None, DA_MAPS, DA_HEAD_DIM, PAGE_SIZE),
                            lambda b, p, pt, f=page_of(j): (layer, f(b, p, pt), 0, 0, 0)) for j in range(pp)]
    v_specs = [pl.BlockSpec((None, None, PAGE_SIZE * DA_HEADS, DA_VALUE_DIM),
                            lambda b, p, pt, f=page_of(j): (layer, f(b, p, pt), 0, 0)) for j in range(pp)]
    grid_spec = pltpu.PrefetchScalarGridSpec(
        num_scalar_prefetch=1,
        grid=(nb, n_steps + 1),
        in_specs=[seq4((DA_MAPS, SUBLANES, DA_HEAD_DIM))] + k_specs + v_specs + [
                  seq4((SUBLANES, DA_MAPS, DA_HEAD_DIM)), seq4((SUBLANES, DA_HEADS, DA_VALUE_DIM)),
                  vec, vec, vec, vec, pl.BlockSpec((1, LANES), lambda b, p, pt: (0, 0))],
        out_specs=pl.BlockSpec((1, SUBLANES, DA_HEADS * DA_VALUE_DIM), lambda b, p, pt: (b, 0, 0)),
        scratch_shapes=[pltpu.VMEM((rows, 1), F32), pltpu.VMEM((rows, 1), F32), pltpu.VMEM((rows, LANES), F32)],
    )
    return pl.pallas_call(
        functools.partial(_attn_sample_kernel, n_steps=n_steps, pp=pp, n_new=n_new, lam_init=lam_init),
        grid_spec=grid_spec,
        out_shape=jax.ShapeDtypeStruct((nb, SUBLANES, DA_HEADS * DA_VALUE_DIM), F32),
        compiler_params=_params(("parallel", "arbitrary")),
        name="attn_sample",
    )(page_table, q, *([cache_kt] * pp), *([cache_v2] * pp), k_new, v_new, *lam_vecs, subln.reshape(1, LANES))


def _split_bf16(x):
    hi = x.astype(BF16)
    lo = (x - hi.astype(F32)).astype(BF16)
    return hi, lo


def _dot_hilo(a, x):
    hi, lo = _split_bf16(x)
    return jnp.dot(a, hi, preferred_element_type=F32) + jnp.dot(a, lo, preferred_element_type=F32)


def _head_sums(x, seg):
    n = x.shape[0]
    n_p = x.shape[1] // LANES
    xs = jnp.concatenate([x[:, p * LANES:(p + 1) * LANES] for p in range(n_p)], axis=0)
    hi, lo = _split_bf16(xs)
    out = jnp.dot(hi, seg, preferred_element_type=F32) + jnp.dot(lo, seg, preferred_element_type=F32)
    return jnp.concatenate([out[p * n:(p + 1) * n] for p in range(n_p)], axis=1)


def _rwkv_kernel(*refs, chunk, seq, layer0):
    if layer0:
        (zr_ref, sh0_ref, s0_ref, mu_ref, w0_ref, w2_ref, a0_ref, a2_ref, g2_ref,
         kk_ref, ka_ref, rk_ref, lnw_ref, lnb_ref,
         y_ref, sout_ref, shout_ref, vf_out_ref, sbd_scr, buf_scr) = refs
    else:
        (zr_ref, sh0_ref, s0_ref, vf_ref, mu_ref, w0_ref, w2_ref, a0_ref, a2_ref, g2_ref,
         v0_ref, v1_ref, v2_ref, kk_ref, ka_ref, rk_ref, lnw_ref, lnb_ref,
         y_ref, sout_ref, shout_ref, sbd_scr, buf_scr) = refs
    C = chunk
    W = RW_WIDTH
    nb = zr_ref.shape[0]
    N = nb * C
    P = nb * RW_PAIRS
    cols = zr_ref.shape[2]
    c_idx = pl.program_id(1)
    n_chunks = pl.num_programs(1)
    zero64 = jnp.zeros((RW_HEAD, RW_HEAD), F32)

    @pl.when(c_idx == 0)
    def _():
        for b in range(nb):
            for p in range(RW_PAIRS):
                top = jnp.concatenate([s0_ref[b, 2 * p], zero64], axis=1)
                bot = jnp.concatenate([zero64, s0_ref[b, 2 * p + 1]], axis=1)
                sbd_scr[b * RW_PAIRS + p] = jnp.concatenate([top, bot], axis=0)
        buf_scr[:, SUBLANES - 1:SUBLANES, :] = sh0_ref[...]

    zr3 = zr_ref[...]
    tpos = c_idx * C + lax.broadcasted_iota(jnp.int32, (nb, C, 1), 1)
    valid3 = tpos < seq
    zr3 = jnp.where(valid3, zr3, 0.0)
    buf_scr[:, SUBLANES:SUBLANES + C, :] = zr3
    prev3 = buf_scr[:, SUBLANES - 1:SUBLANES - 1 + C, :]
    buf_scr[:, SUBLANES - 1:SUBLANES, :] = zr3[:, C - 1:C, :]
    valid = valid3.reshape(N, 1)
    zr = zr3.reshape(N, cols)
    m = jnp.where(valid, zr + (prev3.reshape(N, cols) - zr) * mu_ref[...], 0.0)

    r = m[:, 0:W]
    k = m[:, W:2 * W]
    v = m[:, 2 * W:3 * W]
    x2 = m[:, 3 * W:3 * W + LANES]
    gd = m[:, 3 * W + LANES:3 * W + 2 * LANES]

    wpre = w0_ref[...] + _bdot(jnp.tanh(x2), w2_ref[...])
    nw = -wpre
    softplus = jnp.maximum(nw, 0.0) + jnp.log(1.0 + jnp.exp(-jnp.abs(nw)))
    lw = jnp.where(valid, -jnp.exp(-softplus - 0.5), 0.0)
    a = _sigmoid(a0_ref[...] + _bdot(x2, a2_ref[...]))
    g = _bdot(_sigmoid(gd), g2_ref[...])
    if layer0:
        vf_out_ref[...] = v.reshape(nb, C, W)
    else:
        vf = jnp.where(valid, vf_ref[...].reshape(N, W), 0.0)
        v = v + (vf - v) * _sigmoid(v0_ref[...] + _bdot(_bdot(v, v1_ref[...]), v2_ref[...]))

    lane = lax.broadcasted_iota(jnp.int32, (LANES, LANES), 1)
    sub = lax.broadcasted_iota(jnp.int32, (LANES, LANES), 0)
    seg = (lane // RW_HEAD == sub // RW_HEAD).astype(BF16)

    kkv = k * kk_ref[...]
    kkn = kkv * lax.rsqrt(jnp.maximum(_head_sums(kkv * kkv, seg), 1e-24))
    k2 = k * (1.0 + (a - 1.0) * ka_ref[...])
    aa = -kkn
    bb = kkn * a
    bonus = _head_sums(r * k2 * rk_ref[...], seg)

    tr = lax.broadcasted_iota(jnp.int32, (N, N), 0)
    tc = lax.broadcasted_iota(jnp.int32, (N, N), 1)
    tri = jnp.logical_and(tr // C == tc // C, tr >= tc).astype(BF16)
    cl = _dot_hilo(tri, lw)
    cl3 = cl.reshape(nb, C, W)
    cl_end3 = cl3[:, C - 1:C, :]
    p_in = jnp.exp(cl)
    p_ex = jnp.exp(cl - lw)
    p_inv = jnp.exp(-cl)
    p_end = jnp.exp(cl_end3 - cl3).reshape(N, W)
    p_tot3 = jnp.exp(cl_end3)

    lane_c = lax.broadcasted_iota(jnp.int32, (1, C, LANES), 2)
    m1 = (lane_c < RW_HEAD).astype(F32)
    m2 = 1.0 - m1

    def pairs(x):
        return jnp.stack([x[b * C:(b + 1) * C, p * LANES:(p + 1) * LANES]
                          for b in range(nb) for p in range(RW_PAIRS)], axis=0)

    def stack(x):
        xp = pairs(x)
        return jnp.concatenate([xp * m1, xp * m2], axis=1)

    rr = lax.broadcasted_iota(jnp.int32, (2 * C, 2 * C), 0)
    cc = lax.broadcasted_iota(jnp.int32, (2 * C, 2 * C), 1)
    same = (rr // C) == (cc // C)
    strict = jnp.logical_and(same, (rr % C) > (cc % C))
    incl = jnp.logical_and(same, (rr % C) >= (cc % C))
    base = 8
    eye = (rr == cc).astype(F32)
    base_mask = jnp.logical_and(rr // base == cc // base, rr > cc)
    merge_masks = []
    kb = base
    while kb < C:
        merge_masks.append(jnp.logical_and(rr // (2 * kb) == cc // (2 * kb), rr // kb == cc // kb + 1))
        kb *= 2

    s_bd = sbd_scr[...]
    ar = jnp.concatenate([stack(aa * p_ex), stack(r * p_in)], axis=1).astype(BF16)
    bk = jnp.concatenate([stack(bb * p_inv), stack(k2 * p_inv)], axis=1).astype(BF16)
    vm = stack(v)
    mat = _bmm_nt(ar, bk)
    l_ab = jnp.where(strict, mat[:, :2 * C, :2 * C], 0.0)
    a_ak = jnp.where(strict, mat[:, :2 * C, 2 * C:], 0.0)
    a_rb = jnp.where(incl, mat[:, 2 * C:, :2 * C], 0.0)
    a_rk = jnp.where(incl, mat[:, 2 * C:, 2 * C:], 0.0)
    gs = _bmm_nt(ar, s_bd)
    l8 = jnp.where(base_mask, l_ab, 0.0)
    inv = eye + l8
    lp = l8
    for _ in range(2):
        lp = _bmm(lp, lp)
        inv = inv + _bmm(lp, inv)
    for em in merge_masks:
        inv = inv + _bmm(inv, _bmm(jnp.where(em, l_ab, 0.0), inv))
    x = _bmm(inv, gs[:, :2 * C] + _bmm(a_ak, vm))
    uv = jnp.concatenate([x, vm], axis=1)
    ym = gs[:, 2 * C:] + _bmm(jnp.concatenate([a_rb, a_rk], axis=2), uv)
    yp = ym[:, :C] + ym[:, C:]
    bkh = jnp.concatenate([stack(bb * p_end), stack(k2 * p_end)], axis=1)
    uvt = jnp.stack([uv[i].T for i in range(P)], axis=0)
    p_tot = jnp.stack([p_tot3[b, :, p * LANES:(p + 1) * LANES]
                       for b in range(nb) for p in range(RW_PAIRS)], axis=0)
    s_new = s_bd * p_tot + _bmm(uvt, bkh)
    sbd_scr[...] = s_new
    y = jnp.concatenate(
        [jnp.concatenate([yp[b * RW_PAIRS + p] for p in range(RW_PAIRS)], axis=1) for b in range(nb)], axis=0)

    inv_n = 1.0 / RW_HEAD
    mean = _head_sums(y, seg) * inv_n
    d = y - mean
    var = _head_sums(d * d, seg) * inv_n
    yn = d * lax.rsqrt(var + LNX_EPS) * lnw_ref[...] + lnb_ref[...]
    y_ref[...] = ((yn + bonus * v) * g).reshape(nb, C, W).astype(y_ref.dtype)

    @pl.when(c_idx == n_chunks - 1)
    def _():
        last = (seq - 1) % C
        shout_ref[...] = zr3[:, last:last + 1, :]
        for b in range(nb):
            for p in range(RW_PAIRS):
                blk = s_new[b * RW_PAIRS + p]
                sout_ref[b, 2 * p] = blk[0:RW_HEAD, 0:RW_HEAD]
                sout_ref[b, 2 * p + 1] = blk[RW_HEAD:, RW_HEAD:]


def _rwkv(zr, shift0, wkv0, v_first, prm, *, chunk, seq, nb, layer0):
    b, lpad, cols = zr.shape
    assert b % nb == 0
    n_chunks = pl.cdiv(lpad, chunk)
    W = RW_WIDTH
    row = lambda n: pl.BlockSpec((1, n), lambda bi, c: (0, 0))
    full = lambda s: pl.BlockSpec(s, lambda bi, c: (0, 0))
    tok = lambda n: pl.BlockSpec((nb, chunk, n), lambda bi, c: (bi, c, 0))
    state = pl.BlockSpec((nb, RW_HEADS, RW_HEAD, RW_HEAD), lambda bi, c: (bi, 0, 0, 0))
    shift = pl.BlockSpec((nb, 1, cols), lambda bi, c: (bi, 0, 0))
    in_specs = [tok(cols), shift, state]
    args = [zr, shift0.reshape(b, 1, cols), wkv0]
    if not layer0:
        in_specs.append(tok(W))
        args.append(v_first)
    in_specs += [row(cols), row(W), full((LANES, W)), row(W), full((LANES, W)), full((GATE_LORA, W))]
    args += [prm['mu'], prm['w0'], prm['w2'], prm['a0'], prm['a2'], prm['g2']]
    if not layer0:
        in_specs += [row(W), full((W, LANES)), full((LANES, W))]
        args += [prm['v0'], prm['v1'], prm['v2']]
    in_specs += [row(W)] * 5
    args += [prm['k_k'], prm['k_a'], prm['r_k'], prm['ln_w'], prm['ln_b']]
    out_shape = [jax.ShapeDtypeStruct((b, lpad, W), BF16),
                 jax.ShapeDtypeStruct((b, RW_HEADS, RW_HEAD, RW_HEAD), F32),
                 jax.ShapeDtypeStruct((b, 1, cols), F32)]
    out_specs = [tok(W), state, shift]
    if layer0:
        out_shape.append(jax.ShapeDtypeStruct((b, lpad, W), F32))
        out_specs.append(tok(W))
    outs = pl.pallas_call(
        functools.partial(_rwkv_kernel, chunk=chunk, seq=seq, layer0=layer0),
        grid=(b // nb, n_chunks),
        in_specs=in_specs,
        out_specs=out_specs,
        out_shape=out_shape,
        scratch_shapes=[pltpu.VMEM((nb * RW_PAIRS, LANES, LANES), F32),
                        pltpu.VMEM((nb, chunk + SUBLANES, cols), F32)],
        compiler_params=_params(("parallel", "arbitrary")),
        name="rwkv",
    )(*args)
    if layer0:
        y, s_out, sh_out, vf = outs
    else:
        (y, s_out, sh_out), vf = outs, v_first
    return y, s_out, sh_out.reshape(b, cols), vf


def _rms(x, g):
    ms = jnp.mean(x * x, axis=-1, keepdims=True)
    return x * lax.rsqrt(ms + 1e-6) * g


def _merge_kernel(att_ref, rw_ref, ga_ref, gb_ref, x_ref, wa_ref, wb_ref, wo_ref, n2_ref, o_ref):
    ya = _bdot(att_ref[...], wa_ref[...])
    yb = _bdot(rw_ref[...], wb_ref[...])
    merged = _sigmoid(ga_ref[...].astype(F32)) * ya + _sigmoid(gb_ref[...].astype(F32)) * yb
    mix = _bdot(merged, wo_ref[...])
    o_ref[...] = x_ref[...] + _rms(mix, n2_ref[...])


def _merge(att, rw, gates, x, wa, wb, wo, n2, *, tm):
    t, d = x.shape
    assert t % tm == 0
    tokb = lambda j: pl.BlockSpec((tm, d), lambda i: (i, j))
    wspec = pl.BlockSpec((d, d), lambda i: (0, 0))
    return pl.pallas_call(
        _merge_kernel,
        grid=(t // tm,),
        in_specs=[tokb(0), tokb(0), tokb(0), tokb(1), tokb(0), wspec, wspec, wspec,
                  pl.BlockSpec((1, d), lambda i: (0, 0))],
        out_specs=tokb(0),
        out_shape=jax.ShapeDtypeStruct((t, d), F32),
        compiler_params=_params(("parallel",)),
        name="merge",
    )(att, rw, gates, gates, x, wa, wb, wo, n2.reshape(1, d))


def _ffn_kernel(x_ref, n3_ref, wu_ref, wd_ref, n4_ref, o_ref, *, n_split):
    x = x_ref[...]
    h = _rms(x, n3_ref[...]).astype(BF16)
    dff = wu_ref.shape[1]
    step = dff // n_split
    f = jnp.zeros(x.shape, F32)
    for s in range(n_split):
        u = jnp.dot(h, wu_ref[:, s * step:(s + 1) * step], preferred_element_type=F32)
        u = jnp.square(jnp.maximum(u, 0.0))
        f = f + jnp.dot(u.astype(BF16), wd_ref[s * step:(s + 1) * step, :], preferred_element_type=F32)
    o_ref[...] = x + _rms(f, n4_ref[...])


def _ffn(x, n3, wu, wd, n4, *, tm):
    t, d = x.shape
    dff = wu.shape[1]
    assert t % tm == 0
    tokb = pl.BlockSpec((tm, d), lambda i: (i, 0))
    vec = pl.BlockSpec((1, d), lambda i: (0, 0))
    return pl.pallas_call(
        functools.partial(_ffn_kernel, n_split=4),
        grid=(t // tm,),
        in_specs=[tokb, vec, pl.BlockSpec((d, dff), lambda i: (0, 0)), pl.BlockSpec((dff, d), lambda i: (0, 0)), vec],
        out_specs=tokb,
        out_shape=jax.ShapeDtypeStruct((t, d), F32),
        compiler_params=_params(("parallel",)),
        name="ffn",
    )(x, n3.reshape(1, d), wu, wd, n4.reshape(1, d))


def _pick_tile(n, candidates):
    for c in candidates:
        if n % c == 0:
            return c
    return n


def kernel(x_prompt, x_sample, cache_k, cache_v, state_wkv, state_shift, page_table, meta_tokens, norm_pre_mix, norm_post_mix, norm_pre_ffn, norm_post_ffn, w_in, lambda_q1, lambda_k1, lambda_q2, lambda_k2, da_subln, rw_mu, rw_w0, rw_w2, rw_a0, rw_a2, rw_g2, rw_v0, rw_v1, rw_v2, rw_k_k, rw_k_a, rw_r_k, rw_ln_w, rw_ln_b, w_branch_a, w_branch_b, w_out, w_up, w_down):
    depth = w_in.shape[0]
    b, seq_in, d = x_prompt.shape
    db, ds, _ = x_sample.shape
    seq = seq_in + N_META
    n_past = page_table.shape[1] * PAGE_SIZE
    qk_w = DA_MAPS * DA_HEAD_DIM
    da_w = DA_HEADS * DA_VALUE_DIM
    assert ds <= SUBLANES

    xp = jnp.concatenate([jnp.broadcast_to(meta_tokens[None], (b, N_META, d)), x_prompt], axis=1)
    xs = x_sample.reshape(1, db * ds, d)
    tabs_p = _rope_tables(jnp.arange(seq))
    tabs_s = _rope_tables(n_past + jnp.arange(db * ds) % ds)
    tm_p = _pick_tile(seq, (688, 344, 72, 48, 16))
    tm_s = _pick_tile(db * ds, (512, 256, 128, 64, 32, 16, 8))
    tm_flat_p = _pick_tile(b * seq, (384, 256, 128, 64, 48, 32, 16, 8))
    attn_tq = 128
    chunk_p = 64
    chunk_s = SUBLANES
    nb_s = _pick_tile(db, (8, 4, 2, 1))

    cache_kt = cache_k.transpose(0, 1, 3, 4, 2)
    cache_v2 = cache_v.reshape(depth, cache_v.shape[1], PAGE_SIZE * DA_HEADS, DA_VALUE_DIM)
    w_in_b = w_in.astype(BF16)
    zeros_lora = jnp.zeros((DECAY_LORA, RW_WIDTH), BF16)
    wkv0_p = jnp.zeros((b, RW_HEADS, RW_HEAD, RW_HEAD), F32)
    shift0_p = jnp.zeros((b, RW_COLS), F32)
    q_scale = DA_HEAD_DIM ** -0.5 * LOG2E
    pad_tok = lambda t: jnp.pad(t, ((0, 0), (0, SUBLANES - ds)) + ((0, 0),) * (t.ndim - 2))

    vf_p = vf_s = None
    outs = [[] for _ in range(8)]
    for l in range(depth):
        o = 0
        wq = w_in_b[l, :, o:o + qk_w]; o += qk_w
        wk = w_in_b[l, :, o:o + qk_w]; o += qk_w
        wv = w_in_b[l, :, o:o + da_w]; o += da_w
        wz = w_in_b[l, :, o:o + RW_COLS]; o += RW_COLS
        wg = w_in_b[l, :, o:o + 2 * d]
        row = lambda t: t.reshape(1, -1).astype(F32)
        prm = {'mu': row(rw_mu[l]), 'w0': row(rw_w0[l]), 'a0': row(rw_a0[l]),
               'w2': jnp.concatenate([rw_w2[l].astype(BF16), zeros_lora], axis=0),
               'a2': jnp.concatenate([zeros_lora, rw_a2[l].astype(BF16)], axis=0),
               'g2': rw_g2[l].astype(BF16),
               'k_k': row(rw_k_k[l]), 'k_a': row(rw_k_a[l]), 'r_k': row(rw_r_k[l]),
               'ln_w': row(rw_ln_w[l]), 'ln_b': row(rw_ln_b[l])}
        if l > 0:
            prm['v0'] = row(rw_v0[l - 1])
            prm['v1'] = jnp.pad(rw_v1[l - 1].astype(BF16), ((0, 0), (0, LANES - MV_LORA)))
            prm['v2'] = jnp.pad(rw_v2[l - 1].astype(BF16), ((0, LANES - MV_LORA), (0, 0)))
        lam_init = 0.8 - 0.6 * math.exp(-0.3 * l)
        lam_vecs = [t[l].reshape(1, DA_HEAD_DIM).astype(F32) for t in (lambda_q1, lambda_k1, lambda_q2, lambda_k2)]
        wa, wb, wo = w_branch_a[l].astype(BF16), w_branch_b[l].astype(BF16), w_out[l].astype(BF16)
        wu, wd = w_up[l].astype(BF16), w_down[l].astype(BF16)
        n1 = norm_pre_mix[l]

        def project(x3, tm, tabs, q_dtype):
            q = _norm_matmul(x3, n1, wq, tm=tm, tn=qk_w, out_dtype=q_dtype, rope_tabs=tabs, scale=q_scale)
            k = _norm_matmul(x3, n1, wk, tm=tm, tn=qk_w, out_dtype=F32, rope_tabs=tabs)
            v = _norm_matmul(x3, n1, wv, tm=tm, tn=da_w, out_dtype=F32)
            zr = _norm_matmul(x3, n1, wz, tm=tm, tn=RW_COLS // 2, out_dtype=F32)
            gt = _norm_matmul(x3, n1, wg, tm=tm, tn=d, out_dtype=BF16)
            return q, k, v, zr, gt

        q, k, v, zr, gt = project(xp, tm_p, tabs_p, BF16)
        att = _attn_prompt(q, k, v, lam_vecs, da_subln[l], lam_init, tq=attn_tq)
        rw, wkv_p, sh_p, vf_p = _rwkv(zr, shift0_p, wkv0_p, vf_p, prm, chunk=chunk_p, seq=seq, nb=1,
                                      layer0=(l == 0))
        flat = lambda t: t.reshape(b * seq, -1)
        x2 = _merge(flat(att), flat(rw), flat(gt), flat(xp), wa, wb, wo, norm_post_mix[l], tm=tm_flat_p)
        x2 = _ffn(x2, norm_pre_ffn[l], wu, wd, norm_post_ffn[l], tm=tm_flat_p)
        xp = x2.reshape(b, seq, d)
        outs[0].append(k.reshape(b, seq, DA_MAPS, DA_HEAD_DIM))
        outs[1].append(v.reshape(b, seq, DA_HEADS, DA_VALUE_DIM))
        outs[2].append(wkv_p)
        outs[3].append(sh_p)

        q, k, v, zr, gt = project(xs, tm_s, tabs_s, F32)
        k4 = k.reshape(db, ds, DA_MAPS, DA_HEAD_DIM)
        v4 = v.reshape(db, ds, DA_HEADS, DA_VALUE_DIM)
        q4 = pad_tok(q.reshape(db, ds, DA_MAPS, DA_HEAD_DIM)).transpose(0, 2, 1, 3)
        att8 = _attn_sample(l, q4, cache_kt, cache_v2, page_table, pad_tok(k4), pad_tok(v4), lam_vecs,
                            da_subln[l], lam_init, ds)
        att = att8[:, :ds].reshape(db * ds, da_w)
        rw8, wkv_s, sh_s, vf_s = _rwkv(pad_tok(zr.reshape(db, ds, RW_COLS)), state_shift[l], state_wkv[l], vf_s,
                                        prm, chunk=chunk_s, seq=ds, nb=nb_s, layer0=(l == 0))
        rw = rw8[:, :ds].reshape(db * ds, RW_WIDTH)
        x2 = _merge(att, rw, gt.reshape(db * ds, -1), xs.reshape(db * ds, d), wa, wb, wo, norm_post_mix[l], tm=tm_s)
        x2 = _ffn(x2, norm_pre_ffn[l], wu, wd, norm_post_ffn[l], tm=tm_s)
        xs = x2.reshape(1, db * ds, d)
        outs[4].append(k4)
        outs[5].append(v4)
        outs[6].append(wkv_s)
        outs[7].append(sh_s)

    y_prompt = xp[:, N_META:]
    y_sample = xs.reshape(db, ds, d)
    return (y_prompt, y_sample) + tuple(jnp.stack(o) for o in outs)
```

```python
import functools
import math

import jax
import jax.numpy as jnp
from jax import lax
from jax.experimental import pallas as pl
from jax.experimental.pallas import tpu as pltpu

F32 = jnp.float32
BF16 = jnp.bfloat16

D_MODEL = 1024
N_META = 16
DA_HEADS = 8
DA_MAPS = 2 * DA_HEADS
DA_HEAD_DIM = 64
DA_VALUE_DIM = 128
ROPE_THETA = 10000.0
SUBLN_EPS = 1e-5
RW_HEAD = 64
RW_HEADS = 16
RW_PAIRS = RW_HEADS // 2
RW_WIDTH = 1024
DECAY_LORA = 64
AAA_LORA = 64
GATE_LORA = 128
MV_LORA = 32
RW_COLS = 3 * RW_WIDTH + DECAY_LORA + AAA_LORA + GATE_LORA
LNX_EPS = 64e-5
PAGE_SIZE = 128
LANES = 128
SUBLANES = 8
NEG = -1e30
LOG2E = 1.4426950408889634
VMEM_LIMIT = 52 * 1024 * 1024

NT_DIMS = (((1,), (1,)), ((), ()))


def _bdot(a, b):
    return jnp.dot(a.astype(BF16), b.astype(BF16), preferred_element_type=F32)


def _bmm(a, b):
    return jnp.einsum('pij,pjk->pik', a.astype(BF16), b.astype(BF16), preferred_element_type=F32)


def _bmm_nt(a, b):
    return jnp.einsum('pik,pjk->pij', a.astype(BF16), b.astype(BF16), preferred_element_type=F32)


def _sigmoid(x):
    return 1.0 / (1.0 + jnp.exp(-x))


def _params(sem):
    return pltpu.CompilerParams(dimension_semantics=sem, vmem_limit_bytes=VMEM_LIMIT)


def _norm_matmul_kernel(x_ref, g_ref, w_ref, *rest, rope, scale):
    if rope:
        cos_ref, sa_ref, sb_ref, o_ref = rest
    else:
        (o_ref,) = rest
    x = x_ref[0]
    ms = jnp.mean(x * x, axis=-1, keepdims=True)
    h = (x * lax.rsqrt(ms + 1e-6) * g_ref[...]).astype(BF16)
    acc = jnp.dot(h, w_ref[...], preferred_element_type=F32)
    if rope:
        cos, sa, sb = cos_ref[...], sa_ref[...], sb_ref[...]
        for c in range(acc.shape[1] // LANES):
            blk = acc[:, c * LANES:(c + 1) * LANES]
            out = blk * cos + pltpu.roll(blk, 96, 1) * sa + pltpu.roll(blk, 32, 1) * sb
            if scale != 1.0:
                out = out * scale
            o_ref[0, :, c * LANES:(c + 1) * LANES] = out.astype(o_ref.dtype)
    else:
        o_ref[0] = acc.astype(o_ref.dtype)


def _norm_matmul(x3, g, w, *, tm, tn, out_dtype, rope_tabs=None, scale=1.0):
    bv, lv, d = x3.shape
    n = w.shape[1]
    assert lv % tm == 0 and n % tn == 0
    rope = rope_tabs is not None
    in_specs = [
        pl.BlockSpec((1, tm, d), lambda j, b, i: (b, i, 0)),
        pl.BlockSpec((1, d), lambda j, b, i: (0, 0)),
        pl.BlockSpec((d, tn), lambda j, b, i: (0, j)),
    ]
    args = [x3, g.reshape(1, d), w]
    if rope:
        in_specs += [pl.BlockSpec((tm, LANES), lambda j, b, i: (i, 0))] * 3
        args += list(rope_tabs)
    return pl.pallas_call(
        functools.partial(_norm_matmul_kernel, rope=rope, scale=scale),
        grid=(n // tn, bv, lv // tm),
        in_specs=in_specs,
        out_specs=pl.BlockSpec((1, tm, tn), lambda j, b, i: (b, i, j)),
        out_shape=jax.ShapeDtypeStruct((bv, lv, n), out_dtype),
        compiler_params=_params(("parallel", "parallel", "parallel")),
        name="norm_matmul",
    )(*args)


def _rope_tables(pos):
    half = DA_HEAD_DIM // 2
    inv_freq = ROPE_THETA ** (-jnp.arange(half, dtype=F32) * (2.0 / DA_HEAD_DIM))
    ang = pos.astype(F32)[:, None] * inv_freq[None, :]
    cos, sin = jnp.cos(ang), jnp.sin(ang)
    zero = jnp.zeros_like(sin)
    cos_t = jnp.concatenate([cos] * 4, axis=1)
    sa_t = jnp.concatenate([-sin, zero] * 2, axis=1)
    sb_t = jnp.concatenate([zero, sin] * 2, axis=1)
    return cos_t, sa_t, sb_t


def _lambda(lq1, lk1, lq2, lk2, lam_init):
    s1 = jnp.sum(lq1[...] * lk1[...], axis=-1, keepdims=True)
    s2 = jnp.sum(lq2[...] * lk2[...], axis=-1, keepdims=True)
    return jnp.exp(s1) - jnp.exp(s2) + lam_init


def _attn_prompt_kernel(q_ref, k_ref, v_ref, lq1, lk1, lq2, lk2, sub_ref, o_ref, kb_scr, vb_scr, s_scr, p_scr,
                        *, seq, tq, lam_init):
    lam = _lambda(lq1, lk1, lq2, lk2, lam_init)
    sub = sub_ref[...] * (1.0 - lam_init)
    n_tiles = pl.cdiv(seq, tq)
    pad_rows = kb_scr.shape[0] - seq
    kb_scr[0:seq, :] = k_ref[0].astype(BF16)
    vb_scr[0:seq, :] = v_ref[0].astype(BF16)
    if pad_rows:
        kb_scr[seq:seq + pad_rows, :] = jnp.zeros((pad_rows, LANES), BF16)
        vb_scr[seq:seq + pad_rows, :] = jnp.zeros((pad_rows, LANES), BF16)

    for i in range(n_tiles):
        row0 = i * tq
        rows = min(tq, seq - row0)
        nk = row0 + tq
        q = q_ref[0, row0:row0 + rows, :]
        lo = lax.broadcasted_iota(jnp.int32, (rows, LANES), 1) < DA_HEAD_DIM
        zero = jnp.zeros_like(q)
        q2 = jnp.concatenate([jnp.where(lo, q, zero), jnp.where(lo, zero, q)], axis=0)
        s = lax.dot_general(q2, kb_scr[0:nk, :], NT_DIMS, preferred_element_type=F32)
        qpos = lax.broadcasted_iota(jnp.int32, (2 * rows, tq), 0) % rows
        kpos = lax.broadcasted_iota(jnp.int32, (2 * rows, tq), 1)
        n_blk = nk // LANES
        blocks = [s[:, b * LANES:(b + 1) * LANES] for b in range(n_blk)]
        for b in range(row0 // LANES, n_blk):
            off = b * LANES - row0
            blocks[b] = jnp.where(kpos[:, off:off + LANES] <= qpos[:, off:off + LANES], blocks[b], NEG)
        m_acc = functools.reduce(jnp.maximum, blocks)
        for b in range(n_blk):
            s_scr[0:2 * rows, b * LANES:(b + 1) * LANES] = blocks[b]
        m = jnp.max(m_acc, axis=-1, keepdims=True)
        l_acc = jnp.zeros((2 * rows, LANES), F32)
        for b in range(n_blk):
            p = jnp.exp2(s_scr[0:2 * rows, b * LANES:(b + 1) * LANES] - m)
            l_acc = l_acc + p
            p_scr[0:2 * rows, b * LANES:(b + 1) * LANES] = p.astype(BF16)
        l = jnp.sum(l_acc, axis=-1, keepdims=True)
        acc = jnp.dot(p_scr[0:2 * rows, 0:nk], vb_scr[0:nk, :], preferred_element_type=F32) / l
        o = acc[0:rows] - lam * acc[rows:2 * rows]
        ms = jnp.mean(o * o, axis=-1, keepdims=True)
        o_ref[0, row0:row0 + rows, :] = (o * lax.rsqrt(ms + SUBLN_EPS) * sub).astype(o_ref.dtype)


def _attn_prompt(q, k, v, lam_vecs, subln, lam_init, *, tq):
    b, seq, _ = q.shape
    npad = pl.cdiv(seq, tq) * tq
    vec = pl.BlockSpec((1, DA_HEAD_DIM), lambda bi, h: (0, 0))
    blk = pl.BlockSpec((1, seq, LANES), lambda bi, h: (bi, 0, h))
    return pl.pallas_call(
        functools.partial(_attn_prompt_kernel, seq=seq, tq=tq, lam_init=lam_init),
        grid=(b, DA_HEADS),
        in_specs=[blk, blk, blk, vec, vec, vec, vec, pl.BlockSpec((1, LANES), lambda bi, h: (0, 0))],
        out_specs=blk,
        out_shape=jax.ShapeDtypeStruct((b, seq, DA_HEADS * DA_VALUE_DIM), BF16),
        scratch_shapes=[pltpu.VMEM((npad, LANES), BF16), pltpu.VMEM((npad, LANES), BF16),
                        pltpu.VMEM((2 * tq, npad), F32), pltpu.VMEM((2 * tq, npad), BF16)],
        compiler_params=_params(("parallel", "parallel")),
        name="attn_prompt",
    )(q, k, v, *lam_vecs, subln.reshape(1, LANES))


def _attn_sample_kernel(pt_ref, q_ref, *refs, n_steps, pp, n_new, lam_init):
    del pt_ref
    k_refs, v_refs = refs[:pp], refs[pp:2 * pp]
    kn_ref, vn_ref, lq1, lk1, lq2, lk2, sub_ref, o_ref, m_scr, l_scr, acc_scr = refs[2 * pp:]
    p = pl.program_id(1)
    rpm = SUBLANES
    rows = DA_MAPS * rpm

    @pl.when(p == 0)
    def _():
        m_scr[...] = jnp.full(m_scr.shape, NEG, F32)
        l_scr[...] = jnp.zeros(l_scr.shape, F32)
        acc_scr[...] = jnp.zeros(acc_scr.shape, F32)

    def update(scores, get_v, keep):
        s = jnp.concatenate([scores(q_ref[0, c].astype(BF16), c) for c in range(DA_MAPS)], axis=0)
        if keep is not None:
            s = jnp.where(keep, s, NEG)
        m_old = m_scr[...]
        mn = jnp.maximum(m_old, jnp.max(s, axis=-1, keepdims=True))
        al = jnp.exp2(m_old - mn)
        pr = jnp.exp2(s - mn)
        l_scr[...] = al * l_scr[...] + jnp.sum(pr, axis=-1, keepdims=True)
        m_scr[...] = mn
        pb = pr.astype(BF16)
        pv = jnp.concatenate(
            [jnp.dot(pb[2 * rpm * h:2 * rpm * (h + 1)], get_v(h), preferred_element_type=F32)
             for h in range(DA_HEADS)], axis=0)
        acc_scr[...] = al * acc_scr[...] + pv

    @pl.when(p < n_steps)
    def _():
        update(lambda qc, c: jnp.concatenate(
                   [jnp.dot(qc, k_ref[c].astype(BF16), preferred_element_type=F32) for k_ref in k_refs], axis=1),
               lambda h: jnp.concatenate(
                   [v_ref[pl.ds(h, PAGE_SIZE, stride=DA_HEADS), :].astype(BF16) for v_ref in v_refs], axis=0),
               None)

    @pl.when(p == n_steps)
    def _():
        n_fill = PAGE_SIZE - kn_ref.shape[1]
        kfill = jnp.zeros((n_fill, DA_HEAD_DIM), F32)
        vfill = jnp.zeros((n_fill, DA_VALUE_DIM), F32)
        r = lax.broadcasted_iota(jnp.int32, (rows, PAGE_SIZE), 0) % rpm
        c = lax.broadcasted_iota(jnp.int32, (rows, PAGE_SIZE), 1)
        keep = jnp.logical_and(c <= r, c < n_new)
        update(lambda qc, ci: lax.dot_general(
                   qc, jnp.concatenate([kn_ref[0, :, ci, :], kfill], axis=0).astype(BF16), NT_DIMS,
                   preferred_element_type=F32),
               lambda h: jnp.concatenate([vn_ref[0, :, h, :], vfill], axis=0).astype(BF16), keep)
        lam = _lambda(lq1, lk1, lq2, lk2, lam_init)
        sub = sub_ref[...] * (1.0 - lam_init)
        y = acc_scr[...] / l_scr[...]
        for h in range(DA_HEADS):
            r0 = 2 * rpm * h
            o = y[r0:r0 + rpm] - lam * y[r0 + rpm:r0 + 2 * rpm]
            ms = jnp.mean(o * o, axis=-1, keepdims=True)
            o_ref[0, :, h * LANES:(h + 1) * LANES] = o * lax.rsqrt(ms + SUBLN_EPS) * sub


def _attn_sample(layer, q, cache_kt, cache_v2, page_table, k_new, v_new, lam_vecs, subln, lam_init, n_new):
    nb = q.shape[0]
    n_pages = page_table.shape[1]
    pp = _pick_tile(n_pages, (8, 4, 2, 1))
    n_steps = n_pages // pp
    rows = DA_MAPS * SUBLANES
    vec = pl.BlockSpec((1, DA_HEAD_DIM), lambda b, p, pt: (0, 0))
    seq4 = lambda s: pl.BlockSpec((1,) + s, lambda b, p, pt: (b, 0, 0, 0))

    def page_of(j):
        return lambda b, p, pt: pt[b, jnp.minimum(p, n_steps - 1) * pp + j]

    k_specs = [pl.BlockSpec((None, None, DA_MAPS, DA_HEAD_DIM, PAGE_SIZE),
                            lambda b, p, pt, f=page_of(j): (layer, f(b, p, pt), 0, 0, 0)) for j in range(pp)]
    v_specs = [pl.BlockSpec((None, None, PAGE_SIZE * DA_HEADS, DA_VALUE_DIM),
                            lambda b, p, pt, f=page_of(j): (layer, f(b, p, pt), 0, 0)) for j in range(pp)]
    grid_spec = pltpu.PrefetchScalarGridSpec(
        num_scalar_prefetch=1,
        grid=(nb, n_steps + 1),
        in_specs=[seq4((DA_MAPS, SUBLANES, DA_HEAD_DIM))] + k_specs + v_specs + [
                  seq4((SUBLANES, DA_MAPS, DA_HEAD_DIM)), seq4((SUBLANES, DA_HEADS, DA_VALUE_DIM)),
                  vec, vec, vec, vec, pl.BlockSpec((1, LANES), lambda b, p, pt: (0, 0))],
        out_specs=pl.BlockSpec((1, SUBLANES, DA_HEADS * DA_VALUE_DIM), lambda b, p, pt: (b, 0, 0)),
        scratch_shapes=[pltpu.VMEM((rows, 1), F32), pltpu.VMEM((rows, 1), F32), pltpu.VMEM((rows, LANES), F32)],
    )
    return pl.pallas_call(
        functools.partial(_attn_sample_kernel, n_steps=n_steps, pp=pp, n_new=n_new, lam_init=lam_init),
        grid_spec=grid_spec,
        out_shape=jax.ShapeDtypeStruct((nb, SUBLANES, DA_HEADS * DA_VALUE_DIM), F32),
        compiler_params=_params(("parallel", "arbitrary")),
        name="attn_sample",
    )(page_table, q, *([cache_kt] * pp), *([cache_v2] * pp), k_new, v_new, *lam_vecs, subln.reshape(1, LANES))


def _split_bf16(x):
    hi = x.astype(BF16)
    lo = (x - hi.astype(F32)).astype(BF16)
    return hi, lo


def _dot_hilo(a, x):
    hi, lo = _split_bf16(x)
    return jnp.dot(a, hi, preferred_element_type=F32) + jnp.dot(a, lo, preferred_element_type=F32)


def _head_sums(x, seg):
    n = x.shape[0]
    w = seg.shape[0]
    n_p = x.shape[1] // w
    xs = jnp.concatenate([x[:, p * w:(p + 1) * w] for p in range(n_p)], axis=0)
    out = jnp.dot(xs.astype(BF16), seg, preferred_element_type=F32)
    return jnp.concatenate([out[p * n:(p + 1) * n] for p in range(n_p)], axis=1)


def _tril_inverse(l_ab, eye, base_mask, merge_masks, mm):
    l8 = jnp.where(base_mask, l_ab, 0.0)
    inv = eye + l8
    lp = l8
    for _ in range(2):
        lp = mm(lp, lp)
        inv = inv + mm(lp, inv)
    for em in merge_masks:
        inv = inv + mm(inv, mm(jnp.where(em, l_ab, 0.0), inv))
    return inv


def _rwkv_kernel(*refs, chunk, seq, layer0):
    if layer0:
        (zr_ref, sh0_ref, s0_ref, mu_ref, w0_ref, w2_ref, a0_ref, a2_ref, g2_ref,
         kk_ref, ka_ref, rk_ref, lnw_ref, lnb_ref,
         y_ref, sout_ref, shout_ref, vf_out_ref, sbd_scr, buf_scr) = refs
    else:
        (zr_ref, sh0_ref, s0_ref, vf_ref, mu_ref, w0_ref, w2_ref, a0_ref, a2_ref, g2_ref,
         v0_ref, v1_ref, v2_ref, kk_ref, ka_ref, rk_ref, lnw_ref, lnb_ref,
         y_ref, sout_ref, shout_ref, sbd_scr, buf_scr) = refs
    C = chunk
    W = RW_WIDTH
    nb = zr_ref.shape[0]
    N = nb * C
    P = nb * RW_PAIRS
    cols = zr_ref.shape[2]
    c_idx = pl.program_id(1)
    n_chunks = pl.num_programs(1)
    packed = 2 * C == LANES and P % 2 == 0
    hps = 4 if packed else 2
    G = P // 2 if packed else P
    zero64 = jnp.zeros((RW_HEAD, RW_HEAD), F32)

    def slot_heads(g):
        prs = (g, g + G) if packed else (g,)
        return [(i // RW_PAIRS, 2 * (i % RW_PAIRS) + h) for i in prs for h in range(2)]

    @pl.when(c_idx == 0)
    def _():
        for g in range(G):
            rows_ = []
            for n_, (b, h) in enumerate(slot_heads(g)):
                rows_.append(jnp.concatenate(
                    [s0_ref[b, h] if j == n_ else zero64 for j in range(hps)], axis=1))
            sbd_scr[g] = jnp.concatenate(rows_, axis=0)
        buf_scr[:, SUBLANES - 1:SUBLANES, :] = sh0_ref[...]

    zr3 = zr_ref[...]
    tpos = c_idx * C + lax.broadcasted_iota(jnp.int32, (nb, C, 1), 1)
    valid3 = tpos < seq
    zr3 = jnp.where(valid3, zr3, 0.0)
    buf_scr[:, SUBLANES:SUBLANES + C, :] = zr3
    prev3 = buf_scr[:, SUBLANES - 1:SUBLANES - 1 + C, :]
    buf_scr[:, SUBLANES - 1:SUBLANES, :] = zr3[:, C - 1:C, :]
    valid = valid3.reshape(N, 1)
    zr = zr3.reshape(N, cols)
    m = jnp.where(valid, zr + (prev3.reshape(N, cols) - zr) * mu_ref[...], 0.0)

    r = m[:, 0:W]
    k = m[:, W:2 * W]
    v = m[:, 2 * W:3 * W]
    x2 = m[:, 3 * W:3 * W + LANES]
    gd = m[:, 3 * W + LANES:3 * W + 2 * LANES]

    wpre = w0_ref[...] + _bdot(jnp.tanh(x2), w2_ref[...])
    nw = -wpre
    softplus = jnp.maximum(nw, 0.0) + jnp.log(1.0 + jnp.exp(-jnp.abs(nw)))
    lw = jnp.where(valid, -jnp.exp(-softplus - 0.5), 0.0)
    a = _sigmoid(a0_ref[...] + _bdot(x2, a2_ref[...]))
    g = _bdot(_sigmoid(gd), g2_ref[...])
    if layer0:
        vf_out_ref[...] = v.reshape(nb, C, W)
    else:
        vf = jnp.where(valid, vf_ref[...].reshape(N, W), 0.0)
        v = v + (vf - v) * _sigmoid(v0_ref[...] + _bdot(_bdot(v, v1_ref[...]), v2_ref[...]))

    lane = lax.broadcasted_iota(jnp.int32, (2 * LANES, 2 * LANES), 1)
    sub = lax.broadcasted_iota(jnp.int32, (2 * LANES, 2 * LANES), 0)
    seg = (lane // RW_HEAD == sub // RW_HEAD).astype(BF16)

    kkv = k * kk_ref[...]
    kkn = kkv * lax.rsqrt(jnp.maximum(_head_sums(kkv * kkv, seg), 1e-24))
    k2 = k * (1.0 + (a - 1.0) * ka_ref[...])
    aa = -kkn
    bb = kkn * a
    bonus = _head_sums(r * k2 * rk_ref[...], seg)

    tr = lax.broadcasted_iota(jnp.int32, (N, N), 0)
    tc = lax.broadcasted_iota(jnp.int32, (N, N), 1)
    tri = jnp.logical_and(tr // C == tc // C, tr >= tc).astype(BF16)
    cl = _dot_hilo(tri, lw)
    cl3 = cl.reshape(nb, C, W)
    cl_end3 = cl3[:, C - 1:C, :]
    p_in = jnp.exp(cl)
    p_ex = jnp.exp(cl - lw)
    p_inv = jnp.exp(-cl)
    p_end = jnp.exp(cl_end3 - cl3).reshape(N, W)
    p_tot3 = jnp.exp(cl_end3)

    SW = hps // 2 * LANES
    lane_c = lax.broadcasted_iota(jnp.int32, (1, C, SW), 2)
    m1 = (lane_c % LANES < RW_HEAD).astype(BF16)
    m2 = (lane_c % LANES >= RW_HEAD).astype(BF16)

    def slab(x, i):
        b, p = divmod(i, RW_PAIRS)
        return x[b * C:(b + 1) * C, p * LANES:(p + 1) * LANES]

    def pairs(x):
        if packed:
            return jnp.stack([jnp.concatenate([slab(x, g), slab(x, g + G)], axis=1) for g in range(G)], axis=0)
        return jnp.stack([slab(x, g) for g in range(G)], axis=0)

    def stack(x):
        xp = pairs(x).astype(BF16)
        return jnp.concatenate([xp * m1, xp * m2], axis=1)

    def widen(mask):
        return jnp.concatenate([mask] * (hps // 2), axis=1)

    rr = lax.broadcasted_iota(jnp.int32, (2 * C, 2 * C), 0)
    cc = lax.broadcasted_iota(jnp.int32, (2 * C, 2 * C), 1)
    same = (rr // C) == (cc // C)
    strict = widen(jnp.logical_and(same, (rr % C) > (cc % C)))
    incl = widen(jnp.logical_and(same, (rr % C) >= (cc % C)))
    base = 8
    eye = widen((rr == cc).astype(F32))
    base_mask = widen(jnp.logical_and(rr // base == cc // base, rr > cc))
    merge_masks = []
    kb = base
    while kb < C:
        merge_masks.append(widen(jnp.logical_and(rr // (2 * kb) == cc // (2 * kb), rr // kb == cc // kb + 1)))
        kb *= 2

    def bdiag(x):
        lo, hi = x[..., :LANES], x[..., LANES:]
        z = jnp.zeros_like(lo)
        return jnp.concatenate([jnp.concatenate([lo, z], axis=2), jnp.concatenate([z, hi], axis=2)], axis=1)

    def mm(a_, b_):
        return _bmm(a_, bdiag(b_.astype(BF16))) if packed else _bmm(a_, b_)

    s_bd = sbd_scr[...]
    ar = jnp.concatenate([stack(aa * p_ex), stack(r * p_in)], axis=1)
    bk = jnp.concatenate([stack(bb * p_inv), stack(k2 * p_inv)], axis=1)
    vm = stack(v)
    mats = [_bmm_nt(ar[..., h * LANES:(h + 1) * LANES], bk[..., h * LANES:(h + 1) * LANES])
            for h in range(hps // 2)]
    blk = lambda i, j: jnp.concatenate([mt[:, i * 2 * C:(i + 1) * 2 * C, j * 2 * C:(j + 1) * 2 * C] for mt in mats],
                                       axis=2)
    l_ab = jnp.where(strict, blk(0, 0), 0.0)
    a_ak = jnp.where(strict, blk(0, 1), 0.0)
    a_rb = jnp.where(incl, blk(1, 0), 0.0)
    a_rk = jnp.where(incl, blk(1, 1), 0.0)
    gs = _bmm_nt(ar, s_bd)
    inv = _tril_inverse(l_ab, eye, base_mask, merge_masks, mm)
    x = mm(inv, gs[:, :2 * C] + mm(a_ak, vm))
    ym = gs[:, 2 * C:] + mm(a_rb, x) + mm(a_rk, vm)
    yp = ym[:, :C] + ym[:, C:]
    bkh = jnp.concatenate([stack(bb * p_end), stack(k2 * p_end)], axis=1)
    uv = jnp.concatenate([x, vm.astype(F32)], axis=1)
    upd = _bmm(jnp.stack([uv[i].T for i in range(G)], axis=0), bkh)
    if packed:
        z = jnp.zeros((G, LANES, LANES), F32)
        upd = jnp.concatenate([jnp.concatenate([upd[:, :LANES, :LANES], z], axis=2),
                               jnp.concatenate([z, upd[:, LANES:, LANES:]], axis=2)], axis=1)
    tot = lambda i: p_tot3[i // RW_PAIRS, :, (i % RW_PAIRS) * LANES:(i % RW_PAIRS + 1) * LANES]
    p_tot = jnp.stack([jnp.concatenate([tot(g), tot(g + G)], axis=1) if packed else tot(g) for g in range(G)], axis=0)
    s_new = s_bd * p_tot + upd
    sbd_scr[...] = s_new

    def y_slab(i):
        return yp[i % G][:, (i // G) * LANES:(i // G + 1) * LANES]

    y = jnp.concatenate(
        [jnp.concatenate([y_slab(b * RW_PAIRS + p) for p in range(RW_PAIRS)], axis=1) for b in range(nb)], axis=0)

    inv_n = 1.0 / RW_HEAD
    mean = _head_sums(y, seg) * inv_n
    d = y - mean
    var = _head_sums(d * d, seg) * inv_n
    yn = d * lax.rsqrt(var + LNX_EPS) * lnw_ref[...] + lnb_ref[...]
    y_ref[...] = ((yn + bonus * v) * g).reshape(nb, C, W).astype(y_ref.dtype)

    @pl.when(c_idx == n_chunks - 1)
    def _():
        last = (seq - 1) % C
        shout_ref[...] = zr3[:, last:last + 1, :]
        for g in range(G):
            for n_, (b, h) in enumerate(slot_heads(g)):
                sout_ref[b, h] = s_new[g][n_ * RW_HEAD:(n_ + 1) * RW_HEAD, n_ * RW_HEAD:(n_ + 1) * RW_HEAD]


def _rwkv(zr, shift0, wkv0, v_first, prm, *, chunk, seq, nb, layer0):
    b, lpad, cols = zr.shape
    assert b % nb == 0
    n_chunks = pl.cdiv(lpad, chunk)
    W = RW_WIDTH
    row = lambda n: pl.BlockSpec((1, n), lambda bi, c: (0, 0))
    full = lambda s: pl.BlockSpec(s, lambda bi, c: (0, 0))
    tok = lambda n: pl.BlockSpec((nb, chunk, n), lambda bi, c: (bi, c, 0))
    state = pl.BlockSpec((nb, RW_HEADS, RW_HEAD, RW_HEAD), lambda bi, c: (bi, 0, 0, 0))
    shift = pl.BlockSpec((nb, 1, cols), lambda bi, c: (bi, 0, 0))
    in_specs = [tok(cols), shift, state]
    args = [zr, shift0.reshape(b, 1, cols), wkv0]
    if not layer0:
        in_specs.append(tok(W))
        args.append(v_first)
    in_specs += [row(cols), row(W), full((LANES, W)), row(W), full((LANES, W)), full((GATE_LORA, W))]
    args += [prm['mu'], prm['w0'], prm['w2'], prm['a0'], prm['a2'], prm['g2']]
    if not layer0:
        in_specs += [row(W), full((W, LANES)), full((LANES, W))]
        args += [prm['v0'], prm['v1'], prm['v2']]
    in_specs += [row(W)] * 5
    args += [prm['k_k'], prm['k_a'], prm['r_k'], prm['ln_w'], prm['ln_b']]
    out_shape = [jax.ShapeDtypeStruct((b, lpad, W), BF16),
                 jax.ShapeDtypeStruct((b, RW_HEADS, RW_HEAD, RW_HEAD), F32),
                 jax.ShapeDtypeStruct((b, 1, cols), F32)]
    out_specs = [tok(W), state, shift]
    if layer0:
        out_shape.append(jax.ShapeDtypeStruct((b, lpad, W), F32))
        out_specs.append(tok(W))
    outs = pl.pallas_call(
        functools.partial(_rwkv_kernel, chunk=chunk, seq=seq, layer0=layer0),
        grid=(b // nb, n_chunks),
        in_specs=in_specs,
        out_specs=out_specs,
        out_shape=out_shape,
        scratch_shapes=[pltpu.VMEM((nb * RW_PAIRS // 2, 2 * LANES, 2 * LANES) if 2 * chunk == LANES
                                   else (nb * RW_PAIRS, LANES, LANES), F32),
                        pltpu.VMEM((nb, chunk + SUBLANES, cols), F32)],
        compiler_params=_params(("parallel", "arbitrary")),
        name="rwkv",
    )(*args)
    if layer0:
        y, s_out, sh_out, vf = outs
    else:
        (y, s_out, sh_out), vf = outs, v_first
    return y, s_out, sh_out.reshape(b, cols), vf


def _rms(x, g):
    ms = jnp.mean(x * x, axis=-1, keepdims=True)
    return x * lax.rsqrt(ms + 1e-6) * g


def _merge_kernel(att_ref, rw_ref, ga_ref, gb_ref, x_ref, wa_ref, wb_ref, wo_ref, n2_ref, o_ref):
    ya = _bdot(att_ref[...], wa_ref[...])
    yb = _bdot(rw_ref[...], wb_ref[...])
    merged = _sigmoid(ga_ref[...].astype(F32)) * ya + _sigmoid(gb_ref[...].astype(F32)) * yb
    mix = _bdot(merged, wo_ref[...])
    o_ref[...] = x_ref[...] + _rms(mix, n2_ref[...])


def _merge(att, rw, gates, x, wa, wb, wo, n2, *, tm):
    t, d = x.shape
    assert t % tm == 0
    tokb = lambda j: pl.BlockSpec((tm, d), lambda i: (i, j))
    wspec = pl.BlockSpec((d, d), lambda i: (0, 0))
    return pl.pallas_call(
        _merge_kernel,
        grid=(t // tm,),
        in_specs=[tokb(0), tokb(0), tokb(0), tokb(1), tokb(0), wspec, wspec, wspec,
                  pl.BlockSpec((1, d), lambda i: (0, 0))],
        out_specs=tokb(0),
        out_shape=jax.ShapeDtypeStruct((t, d), F32),
        compiler_params=_params(("parallel",)),
        name="merge",
    )(att, rw, gates, gates, x, wa, wb, wo, n2.reshape(1, d))


def _ffn_kernel(x_ref, n3_ref, wu_ref, wd_ref, n4_ref, o_ref, *, n_split):
    x = x_ref[...]
    h = _rms(x, n3_ref[...]).astype(BF16)
    dff = wu_ref.shape[1]
    step = dff // n_split
    f = jnp.zeros(x.shape, F32)
    for s in range(n_split):
        u = jnp.dot(h, wu_ref[:, s * step:(s + 1) * step], preferred_element_type=F32)
        u = jnp.square(jnp.maximum(u, 0.0))
        f = f + jnp.dot(u.astype(BF16), wd_ref[s * step:(s + 1) * step, :], preferred_element_type=F32)
    o_ref[...] = x + _rms(f, n4_ref[...])


def _ffn(x, n3, wu, wd, n4, *, tm):
    t, d = x.shape
    dff = wu.shape[1]
    assert t % tm == 0
    tokb = pl.BlockSpec((tm, d), lambda i: (i, 0))
    vec = pl.BlockSpec((1, d), lambda i: (0, 0))
    return pl.pallas_call(
        functools.partial(_ffn_kernel, n_split=4),
        grid=(t // tm,),
        in_specs=[tokb, vec, pl.BlockSpec((d, dff), lambda i: (0, 0)), pl.BlockSpec((dff, d), lambda i: (0, 0)), vec],
        out_specs=tokb,
        out_shape=jax.ShapeDtypeStruct((t, d), F32),
        compiler_params=_params(("parallel",)),
        name="ffn",
    )(x, n3.reshape(1, d), wu, wd, n4.reshape(1, d))


def _pick_tile(n, candidates):
    for c in candidates:
        if n % c == 0:
            return c
    return n


def kernel(x_prompt, x_sample, cache_k, cache_v, state_wkv, state_shift, page_table, meta_tokens, norm_pre_mix, norm_post_mix, norm_pre_ffn, norm_post_ffn, w_in, lambda_q1, lambda_k1, lambda_q2, lambda_k2, da_subln, rw_mu, rw_w0, rw_w2, rw_a0, rw_a2, rw_g2, rw_v0, rw_v1, rw_v2, rw_k_k, rw_k_a, rw_r_k, rw_ln_w, rw_ln_b, w_branch_a, w_branch_b, w_out, w_up, w_down):
    depth = w_in.shape[0]
    b, seq_in, d = x_prompt.shape
    db, ds, _ = x_sample.shape
    seq = seq_in + N_META
    n_past = page_table.shape[1] * PAGE_SIZE
    qk_w = DA_MAPS * DA_HEAD_DIM
    da_w = DA_HEADS * DA_VALUE_DIM
    assert ds <= SUBLANES

    xp = jnp.concatenate([jnp.broadcast_to(meta_tokens[None], (b, N_META, d)), x_prompt], axis=1)
    xs = x_sample.reshape(1, db * ds, d)
    tabs_p = _rope_tables(jnp.arange(seq))
    tabs_s = _rope_tables(n_past + jnp.arange(db * ds) % ds)
    tm_p = _pick_tile(seq, (688, 344, 72, 48, 16))
    tm_s = _pick_tile(db * ds, (512, 256, 128, 64, 32, 16, 8))
    tm_flat_p = _pick_tile(b * seq, (384, 256, 128, 64, 48, 32, 16, 8))
    attn_tq = 128
    chunk_p = 64
    chunk_s = SUBLANES
    nb_s = _pick_tile(db, (8, 4, 2, 1))

    cache_kt = cache_k.transpose(0, 1, 3, 4, 2)
    cache_v2 = cache_v.reshape(depth, cache_v.shape[1], PAGE_SIZE * DA_HEADS, DA_VALUE_DIM)
    w_in_b = w_in.astype(BF16)
    zeros_lora = jnp.zeros((DECAY_LORA, RW_WIDTH), BF16)
    wkv0_p = jnp.zeros((b, RW_HEADS, RW_HEAD, RW_HEAD), F32)
    shift0_p = jnp.zeros((b, RW_COLS), F32)
    q_scale = DA_HEAD_DIM ** -0.5 * LOG2E
    pad_tok = lambda t: jnp.pad(t, ((0, 0), (0, SUBLANES - ds)) + ((0, 0),) * (t.ndim - 2))

    vf_p = vf_s = None
    outs = [[] for _ in range(8)]
    for l in range(depth):
        o = 0
        wq = w_in_b[l, :, o:o + qk_w]; o += qk_w
        wk = w_in_b[l, :, o:o + qk_w]; o += qk_w
        wv = w_in_b[l, :, o:o + da_w]; o += da_w
        wz = w_in_b[l, :, o:o + RW_COLS]; o += RW_COLS
        wg = w_in_b[l, :, o:o + 2 * d]
        row = lambda t: t.reshape(1, -1).astype(F32)
        prm = {'mu': row(rw_mu[l]), 'w0': row(rw_w0[l]), 'a0': row(rw_a0[l]),
               'w2': jnp.concatenate([rw_w2[l].astype(BF16), zeros_lora], axis=0),
               'a2': jnp.concatenate([zeros_lora, rw_a2[l].astype(BF16)], axis=0),
               'g2': rw_g2[l].astype(BF16),
               'k_k': row(rw_k_k[l]), 'k_a': row(rw_k_a[l]), 'r_k': row(rw_r_k[l]),
               'ln_w': row(rw_ln_w[l]), 'ln_b': row(rw_ln_b[l])}
        if l > 0:
            prm['v0'] = row(rw_v0[l - 1])
            prm['v1'] = jnp.pad(rw_v1[l - 1].astype(BF16), ((0, 0), (0, LANES - MV_LORA)))
            prm['v2'] = jnp.pad(rw_v2[l - 1].astype(BF16), ((0, LANES - MV_LORA), (0, 0)))
        lam_init = 0.8 - 0.6 * math.exp(-0.3 * l)
        lam_vecs = [t[l].reshape(1, DA_HEAD_DIM).astype(F32) for t in (lambda_q1, lambda_k1, lambda_q2, lambda_k2)]
        wa, wb, wo = w_branch_a[l].astype(BF16), w_branch_b[l].astype(BF16), w_out[l].astype(BF16)
        wu, wd = w_up[l].astype(BF16), w_down[l].astype(BF16)
        n1 = norm_pre_mix[l]

        def project(x3, tm, tabs, q_dtype):
            q = _norm_matmul(x3, n1, wq, tm=tm, tn=qk_w, out_dtype=q_dtype, rope_tabs=tabs, scale=q_scale)
            k = _norm_matmul(x3, n1, wk, tm=tm, tn=qk_w, out_dtype=F32, rope_tabs=tabs)
            v = _norm_matmul(x3, n1, wv, tm=tm, tn=da_w, out_dtype=F32)
            zr = _norm_matmul(x3, n1, wz, tm=tm, tn=RW_COLS // 2, out_dtype=F32)
            gt = _norm_matmul(x3, n1, wg, tm=tm, tn=d, out_dtype=BF16)
            return q, k, v, zr, gt

        q, k, v, zr, gt = project(xp, tm_p, tabs_p, BF16)
        att = _attn_prompt(q, k, v, lam_vecs, da_subln[l], lam_init, tq=attn_tq)
        rw, wkv_p, sh_p, vf_p = _rwkv(zr, shift0_p, wkv0_p, vf_p, prm, chunk=chunk_p, seq=seq, nb=1,
                                      layer0=(l == 0))
        flat = lambda t: t.reshape(b * seq, -1)
        x2 = _merge(flat(att), flat(rw), flat(gt), flat(xp), wa, wb, wo, norm_post_mix[l], tm=tm_flat_p)
        x2 = _ffn(x2, norm_pre_ffn[l], wu, wd, norm_post_ffn[l], tm=tm_flat_p)
        xp = x2.reshape(b, seq, d)
        outs[0].append(k.reshape(b, seq, DA_MAPS, DA_HEAD_DIM))
        outs[1].append(v.reshape(b, seq, DA_HEADS, DA_VALUE_DIM))
        outs[2].append(wkv_p)
        outs[3].append(sh_p)

        q, k, v, zr, gt = project(xs, tm_s, tabs_s, F32)
        k4 = k.reshape(db, ds, DA_MAPS, DA_HEAD_DIM)
        v4 = v.reshape(db, ds, DA_HEADS, DA_VALUE_DIM)
        q4 = pad_tok(q.reshape(db, ds, DA_MAPS, DA_HEAD_DIM)).transpose(0, 2, 1, 3)
        att8 = _attn_sample(l, q4, cache_kt, cache_v2, page_table, pad_tok(k4), pad_tok(v4), lam_vecs,
                            da_subln[l], lam_init, ds)
        att = att8[:, :ds].reshape(db * ds, da_w)
        rw8, wkv_s, sh_s, vf_s = _rwkv(pad_tok(zr.reshape(db, ds, RW_COLS)), state_shift[l], state_wkv[l], vf_s,
                                        prm, chunk=chunk_s, seq=ds, nb=nb_s, layer0=(l == 0))
        rw = rw8[:, :ds].reshape(db * ds, RW_WIDTH)
        x2 = _merge(att, rw, gt.reshape(db * ds, -1), xs.reshape(db * ds, d), wa, wb, wo, norm_post_mix[l], tm=tm_s)
        x2 = _ffn(x2, norm_pre_ffn[l], wu, wd, norm_post_ffn[l], tm=tm_s)
        xs = x2.reshape(1, db * ds, d)
        outs[4].append(k4)
        outs[5].append(v4)
        outs[6].append(wkv_s)
        outs[7].append(sh_s)

    y_prompt = xp[:, N_META:]
    y_sample = xs.reshape(db, ds, d)
    return (y_prompt, y_sample) + tuple(jnp.stack(o) for o in outs)
```

```python
import functools
import math

import jax
import jax.numpy as jnp
from jax import lax
from jax.experimental import pallas as pl
from jax.experimental.pallas import tpu as pltpu

F32 = jnp.float32
BF16 = jnp.bfloat16

D_MODEL = 1024
N_META = 16
DA_HEADS = 8
DA_MAPS = 2 * DA_HEADS
DA_HEAD_DIM = 64
DA_VALUE_DIM = 128
ROPE_THETA = 10000.0
SUBLN_EPS = 1e-5
RW_HEAD = 64
RW_HEADS = 16
RW_PAIRS = RW_HEADS // 2
RW_WIDTH = 1024
DECAY_LORA = 64
AAA_LORA = 64
GATE_LORA = 128
MV_LORA = 32
RW_COLS = 3 * RW_WIDTH + DECAY_LORA + AAA_LORA + GATE_LORA
LNX_EPS = 64e-5
PAGE_SIZE = 128
LANES = 128
SUBLANES = 8
NEG = -1e30
LOG2E = 1.4426950408889634
VMEM_LIMIT = 52 * 1024 * 1024

NT_DIMS = (((1,), (1,)), ((), ()))


def _bdot(a, b):
    return jnp.dot(a.astype(BF16), b.astype(BF16), preferred_element_type=F32)


def _bmm(a, b):
    return jnp.einsum('pij,pjk->pik', a.astype(BF16), b.astype(BF16), preferred_element_type=F32)


def _bmm_nt(a, b):
    return jnp.einsum('pik,pjk->pij', a.astype(BF16), b.astype(BF16), preferred_element_type=F32)


def _sigmoid(x):
    return 1.0 / (1.0 + jnp.exp(-x))


def _params(sem):
    return pltpu.CompilerParams(dimension_semantics=sem, vmem_limit_bytes=VMEM_LIMIT)


def _norm_matmul_kernel(x_ref, g_ref, w_ref, *rest, rope, scale):
    if rope:
        cos_ref, sa_ref, sb_ref, o_ref = rest
    else:
        (o_ref,) = rest
    x = x_ref[0]
    ms = jnp.mean(x * x, axis=-1, keepdims=True)
    h = (x * lax.rsqrt(ms + 1e-6) * g_ref[...]).astype(BF16)
    acc = jnp.dot(h, w_ref[...], preferred_element_type=F32)
    if rope:
        cos, sa, sb = cos_ref[...], sa_ref[...], sb_ref[...]
        for c in range(acc.shape[1] // LANES):
            blk = acc[:, c * LANES:(c + 1) * LANES]
            out = blk * cos + pltpu.roll(blk, 96, 1) * sa + pltpu.roll(blk, 32, 1) * sb
            if scale != 1.0:
                out = out * scale
            o_ref[0, :, c * LANES:(c + 1) * LANES] = out.astype(o_ref.dtype)
    else:
        o_ref[0] = acc.astype(o_ref.dtype)


def _norm_matmul(x3, g, w, *, tm, tn, out_dtype, rope_tabs=None, scale=1.0):
    bv, lv, d = x3.shape
    n = w.shape[1]
    assert lv % tm == 0 and n % tn == 0
    rope = rope_tabs is not None
    in_specs = [
        pl.BlockSpec((1, tm, d), lambda j, b, i: (b, i, 0)),
        pl.BlockSpec((1, d), lambda j, b, i: (0, 0)),
        pl.BlockSpec((d, tn), lambda j, b, i: (0, j)),
    ]
    args = [x3, g.reshape(1, d), w]
    if rope:
        in_specs += [pl.BlockSpec((tm, LANES), lambda j, b, i: (i, 0))] * 3
        args += list(rope_tabs)
    return pl.pallas_call(
        functools.partial(_norm_matmul_kernel, rope=rope, scale=scale),
        grid=(n // tn, bv, lv // tm),
        in_specs=in_specs,
        out_specs=pl.BlockSpec((1, tm, tn), lambda j, b, i: (b, i, j)),
        out_shape=jax.ShapeDtypeStruct((bv, lv, n), out_dtype),
        compiler_params=_params(("parallel", "parallel", "parallel")),
        name="norm_matmul",
    )(*args)


def _rope_tables(pos):
    half = DA_HEAD_DIM // 2
    inv_freq = ROPE_THETA ** (-jnp.arange(half, dtype=F32) * (2.0 / DA_HEAD_DIM))
    ang = pos.astype(F32)[:, None] * inv_freq[None, :]
    cos, sin = jnp.cos(ang), jnp.sin(ang)
    zero = jnp.zeros_like(sin)
    cos_t = jnp.concatenate([cos] * 4, axis=1)
    sa_t = jnp.concatenate([-sin, zero] * 2, axis=1)
    sb_t = jnp.concatenate([zero, sin] * 2, axis=1)
    return cos_t, sa_t, sb_t


def _lambda(lq1, lk1, lq2, lk2, lam_init):
    s1 = jnp.sum(lq1[...] * lk1[...], axis=-1, keepdims=True)
    s2 = jnp.sum(lq2[...] * lk2[...], axis=-1, keepdims=True)
    return jnp.exp(s1) - jnp.exp(s2) + lam_init


def _attn_prompt_kernel(q_ref, k_ref, v_ref, lq1, lk1, lq2, lk2, sub_ref, o_ref, kb_scr, vb_scr, s_scr, p_scr,
                        *, seq, tq, lam_init):
    lam = _lambda(lq1, lk1, lq2, lk2, lam_init)
    sub = sub_ref[...] * (1.0 - lam_init)
    n_tiles = pl.cdiv(seq, tq)
    pad_rows = kb_scr.shape[0] - seq
    kb_scr[0:seq, :] = k_ref[0].astype(BF16)
    vb_scr[0:seq, :] = v_ref[0].astype(BF16)
    if pad_rows:
        kb_scr[seq:seq + pad_rows, :] = jnp.zeros((pad_rows, LANES), BF16)
        vb_scr[seq:seq + pad_rows, :] = jnp.zeros((pad_rows, LANES), BF16)

    for i in range(n_tiles):
        row0 = i * tq
        rows = min(tq, seq - row0)
        nk = row0 + tq
        q = q_ref[0, row0:row0 + rows, :]
        lo = lax.broadcasted_iota(jnp.int32, (rows, LANES), 1) < DA_HEAD_DIM
        zero = jnp.zeros_like(q)
        q2 = jnp.concatenate([jnp.where(lo, q, zero), jnp.where(lo, zero, q)], axis=0)
        s = lax.dot_general(q2, kb_scr[0:nk, :], NT_DIMS, preferred_element_type=F32)
        qpos = lax.broadcasted_iota(jnp.int32, (2 * rows, tq), 0) % rows
        kpos = lax.broadcasted_iota(jnp.int32, (2 * rows, tq), 1)
        n_blk = nk // LANES
        blocks = [s[:, b * LANES:(b + 1) * LANES] for b in range(n_blk)]
        for b in range(row0 // LANES, n_blk):
            off = b * LANES - row0
            blocks[b] = jnp.where(kpos[:, off:off + LANES] <= qpos[:, off:off + LANES], blocks[b], NEG)
        m_acc = functools.reduce(jnp.maximum, blocks)
        for b in range(n_blk):
            s_scr[0:2 * rows, b * LANES:(b + 1) * LANES] = blocks[b]
        m = jnp.max(m_acc, axis=-1, keepdims=True)
        l_acc = jnp.zeros((2 * rows, LANES), F32)
        for b in range(n_blk):
            p = jnp.exp2(s_scr[0:2 * rows, b * LANES:(b + 1) * LANES] - m)
            l_acc = l_acc + p
            p_scr[0:2 * rows, b * LANES:(b + 1) * LANES] = p.astype(BF16)
        l = jnp.sum(l_acc, axis=-1, keepdims=True)
        acc = jnp.dot(p_scr[0:2 * rows, 0:nk], vb_scr[0:nk, :], preferred_element_type=F32) / l
        o = acc[0:rows] - lam * acc[rows:2 * rows]
        ms = jnp.mean(o * o, axis=-1, keepdims=True)
        o_ref[0, row0:row0 + rows, :] = (o * lax.rsqrt(ms + SUBLN_EPS) * sub).astype(o_ref.dtype)


def _attn_prompt(q, k, v, lam_vecs, subln, lam_init, *, tq):
    b, seq, _ = q.shape
    npad = pl.cdiv(seq, tq) * tq
    vec = pl.BlockSpec((1, DA_HEAD_DIM), lambda bi, h: (0, 0))
    blk = pl.BlockSpec((1, seq, LANES), lambda bi, h: (bi, 0, h))
    return pl.pallas_call(
        functools.partial(_attn_prompt_kernel, seq=seq, tq=tq, lam_init=lam_init),
        grid=(b, DA_HEADS),
        in_specs=[blk, blk, blk, vec, vec, vec, vec, pl.BlockSpec((1, LANES), lambda bi, h: (0, 0))],
        out_specs=blk,
        out_shape=jax.ShapeDtypeStruct((b, seq, DA_HEADS * DA_VALUE_DIM), BF16),
        scratch_shapes=[pltpu.VMEM((npad, LANES), BF16), pltpu.VMEM((npad, LANES), BF16),
                        pltpu.VMEM((2 * tq, npad), F32), pltpu.VMEM((2 * tq, npad), BF16)],
        compiler_params=_params(("parallel", "parallel")),
        name="attn_prompt",
    )(q, k, v, *lam_vecs, subln.reshape(1, LANES))


def _attn_sample_kernel(pt_ref, q_ref, *refs, n_steps, pp, n_new, lam_init):
    del pt_ref
    k_refs, v_refs = refs[:pp], refs[pp:2 * pp]
    kn_ref, vn_ref, lq1, lk1, lq2, lk2, sub_ref, o_ref, m_scr, l_scr, acc_scr = refs[2 * pp:]
    p = pl.program_id(1)
    rpm = SUBLANES
    rows = DA_MAPS * rpm

    @pl.when(p == 0)
    def _():
        m_scr[...] = jnp.full(m_scr.shape, NEG, F32)
        l_scr[...] = jnp.zeros(l_scr.shape, F32)
        acc_scr[...] = jnp.zeros(acc_scr.shape, F32)

    def update(scores, get_v, keep):
        s = jnp.concatenate([scores(q_ref[0, c].astype(BF16), c) for c in range(DA_MAPS)], axis=0)
        if keep is not None:
            s = jnp.where(keep, s, NEG)
        m_old = m_scr[...]
        mn = jnp.maximum(m_old, jnp.max(s, axis=-1, keepdims=True))
        al = jnp.exp2(m_old - mn)
        pr = jnp.exp2(s - mn)
        l_scr[...] = al * l_scr[...] + jnp.sum(pr, axis=-1, keepdims=True)
        m_scr[...] = mn
        pb = pr.astype(BF16)
        pv = jnp.concatenate(
            [jnp.dot(pb[2 * rpm * h:2 * rpm * (h + 1)], get_v(h), preferred_element_type=F32)
             for h in range(DA_HEADS)], axis=0)
        acc_scr[...] = al * acc_scr[...] + pv

    update(lambda qc, c: jnp.concatenate(
               [jnp.dot(qc, k_ref[c].astype(BF16), preferred_element_type=F32) for k_ref in k_refs], axis=1),
           lambda h: jnp.concatenate(
               [v_ref[pl.ds(h, PAGE_SIZE, stride=DA_HEADS), :].astype(BF16) for v_ref in v_refs], axis=0),
           None)

    @pl.when(p == n_steps - 1)
    def _():
        n_fill = PAGE_SIZE - kn_ref.shape[1]
        kfill = jnp.zeros((n_fill, DA_HEAD_DIM), F32)
        vfill = jnp.zeros((n_fill, DA_VALUE_DIM), F32)
        r = lax.broadcasted_iota(jnp.int32, (rows, PAGE_SIZE), 0) % rpm
        c = lax.broadcasted_iota(jnp.int32, (rows, PAGE_SIZE), 1)
        keep = jnp.logical_and(c <= r, c < n_new)
        update(lambda qc, ci: lax.dot_general(
                   qc, jnp.concatenate([kn_ref[0, :, ci, :], kfill], axis=0).astype(BF16), NT_DIMS,
                   preferred_element_type=F32),
               lambda h: jnp.concatenate([vn_ref[0, :, h, :], vfill], axis=0).astype(BF16), keep)
        lam = _lambda(lq1, lk1, lq2, lk2, lam_init)
        sub = sub_ref[...] * (1.0 - lam_init)
        y = acc_scr[...] / l_scr[...]
        for h in range(DA_HEADS):
            r0 = 2 * rpm * h
            o = y[r0:r0 + rpm] - lam * y[r0 + rpm:r0 + 2 * rpm]
            ms = jnp.mean(o * o, axis=-1, keepdims=True)
            o_ref[0, :, h * LANES:(h + 1) * LANES] = o * lax.rsqrt(ms + SUBLN_EPS) * sub


def _attn_sample(layer, q, cache_kt, cache_v2, page_table, k_new, v_new, lam_vecs, subln, lam_init, n_new):
    nb = q.shape[0]
    n_pages = page_table.shape[1]
    pp = _pick_tile(n_pages, (8, 4, 2, 1))
    n_steps = n_pages // pp
    rows = DA_MAPS * SUBLANES
    vec = pl.BlockSpec((1, DA_HEAD_DIM), lambda b, p, pt: (0, 0))
    seq4 = lambda s: pl.BlockSpec((1,) + s, lambda b, p, pt: (b, 0, 0, 0))

    def page_of(j):
        return lambda b, p, pt: pt[b, p * pp + j]

    k_specs = [pl.BlockSpec((None, None, DA_MAPS, DA_HEAD_DIM, PAGE_SIZE),
                            lambda b, p, pt, f=page_of(j): (layer, f(b, p, pt), 0, 0, 0)) for j in range(pp)]
    v_specs = [pl.BlockSpec((None, None, PAGE_SIZE * DA_HEADS, DA_VALUE_DIM),
                            lambda b, p, pt, f=page_of(j): (layer, f(b, p, pt), 0, 0)) for j in range(pp)]
    grid_spec = pltpu.PrefetchScalarGridSpec(
        num_scalar_prefetch=1,
        grid=(nb, n_steps),
        in_specs=[seq4((DA_MAPS, SUBLANES, DA_HEAD_DIM))] + k_specs + v_specs + [
                  seq4((SUBLANES, DA_MAPS, DA_HEAD_DIM)), seq4((SUBLANES, DA_HEADS, DA_VALUE_DIM)),
                  vec, vec, vec, vec, pl.BlockSpec((1, LANES), lambda b, p, pt: (0, 0))],
        out_specs=pl.BlockSpec((1, SUBLANES, DA_HEADS * DA_VALUE_DIM), lambda b, p, pt: (b, 0, 0)),
        scratch_shapes=[pltpu.VMEM((rows, 1), F32), pltpu.VMEM((rows, 1), F32), pltpu.VMEM((rows, LANES), F32)],
    )
    return pl.pallas_call(
        functools.partial(_attn_sample_kernel, n_steps=n_steps, pp=pp, n_new=n_new, lam_init=lam_init),
        grid_spec=grid_spec,
        out_shape=jax.ShapeDtypeStruct((nb, SUBLANES, DA_HEADS * DA_VALUE_DIM), F32),
        compiler_params=_params(("parallel", "arbitrary")),
        name="attn_sample",
    )(page_table, q, *([cache_kt] * pp), *([cache_v2] * pp), k_new, v_new, *lam_vecs, subln.reshape(1, LANES))


def _split_bf16(x):
    hi = x.astype(BF16)
    lo = (x - hi.astype(F32)).astype(BF16)
    return hi, lo


def _dot_hilo(a, x):
    hi, lo = _split_bf16(x)
    return jnp.dot(a, hi, preferred_element_type=F32) + jnp.dot(a, lo, preferred_element_type=F32)


def _head_sums(x, seg):
    n = x.shape[0]
    w = seg.shape[0]
    n_p = x.shape[1] // w
    xs = jnp.concatenate([x[:, p * w:(p + 1) * w] for p in range(n_p)], axis=0)
    out = jnp.dot(xs.astype(BF16), seg, preferred_element_type=F32)
    return jnp.concatenate([out[p * n:(p + 1) * n] for p in range(n_p)], axis=1)


def _tril_inverse(l_ab, eye, base_mask, merge_masks, mm):
    l8 = jnp.where(base_mask, l_ab, 0.0)
    inv = eye + l8
    lp = l8
    for _ in range(2):
        lp = mm(lp, lp)
        inv = inv + mm(lp, inv)
    for em in merge_masks:
        inv = inv + mm(inv, mm(jnp.where(em, l_ab, 0.0), inv))
    return inv


def _rwkv_kernel(*refs, chunk, seq, layer0):
    if layer0:
        (zr_ref, sh0_ref, s0_ref, mu_ref, w0_ref, w2_ref, a0_ref, a2_ref, g2_ref,
         kk_ref, ka_ref, rk_ref, lnw_ref, lnb_ref,
         y_ref, sout_ref, shout_ref, vf_out_ref, sbd_scr, buf_scr) = refs
    else:
        (zr_ref, sh0_ref, s0_ref, vf_ref, mu_ref, w0_ref, w2_ref, a0_ref, a2_ref, g2_ref,
         v0_ref, v1_ref, v2_ref, kk_ref, ka_ref, rk_ref, lnw_ref, lnb_ref,
         y_ref, sout_ref, shout_ref, sbd_scr, buf_scr) = refs
    C = chunk
    W = RW_WIDTH
    nb = zr_ref.shape[0]
    N = nb * C
    P = nb * RW_PAIRS
    cols = zr_ref.shape[2]
    c_idx = pl.program_id(1)
    n_chunks = pl.num_programs(1)
    packed = 2 * C == LANES and P % 2 == 0
    hps = 4 if packed else 2
    G = P // 2 if packed else P
    zero64 = jnp.zeros((RW_HEAD, RW_HEAD), F32)

    def slot_heads(g):
        prs = (g, g + G) if packed else (g,)
        return [(i // RW_PAIRS, 2 * (i % RW_PAIRS) + h) for i in prs for h in range(2)]

    @pl.when(c_idx == 0)
    def _():
        for g in range(G):
            rows_ = []
            for n_, (b, h) in enumerate(slot_heads(g)):
                rows_.append(jnp.concatenate(
                    [s0_ref[b, h] if j == n_ else zero64 for j in range(hps)], axis=1))
            sbd_scr[g] = jnp.concatenate(rows_, axis=0)
        buf_scr[:, SUBLANES - 1:SUBLANES, :] = sh0_ref[...]

    zr3 = zr_ref[...]
    tpos = c_idx * C + lax.broadcasted_iota(jnp.int32, (nb, C, 1), 1)
    valid3 = tpos < seq
    zr3 = jnp.where(valid3, zr3, 0.0)
    buf_scr[:, SUBLANES:SUBLANES + C, :] = zr3
    prev3 = buf_scr[:, SUBLANES - 1:SUBLANES - 1 + C, :]
    buf_scr[:, SUBLANES - 1:SUBLANES, :] = zr3[:, C - 1:C, :]
    valid = valid3.reshape(N, 1)
    zr = zr3.reshape(N, cols)
    m = jnp.where(valid, zr + (prev3.reshape(N, cols) - zr) * mu_ref[...], 0.0)

    r = m[:, 0:W]
    k = m[:, W:2 * W]
    v = m[:, 2 * W:3 * W]
    x2 = m[:, 3 * W:3 * W + LANES]
    gd = m[:, 3 * W + LANES:3 * W + 2 * LANES]

    wpre = w0_ref[...] + _bdot(jnp.tanh(x2), w2_ref[...])
    nw = -wpre
    softplus = jnp.maximum(nw, 0.0) + jnp.log(1.0 + jnp.exp(-jnp.abs(nw)))
    lw = jnp.where(valid, -jnp.exp(-softplus - 0.5), 0.0)
    a = _sigmoid(a0_ref[...] + _bdot(x2, a2_ref[...]))
    g = _bdot(_sigmoid(gd), g2_ref[...])
    if layer0:
        vf_out_ref[...] = v.reshape(nb, C, W)
    else:
        vf = jnp.where(valid, vf_ref[...].reshape(N, W), 0.0)
        v = v + (vf - v) * _sigmoid(v0_ref[...] + _bdot(_bdot(v, v1_ref[...]), v2_ref[...]))

    lane = lax.broadcasted_iota(jnp.int32, (2 * LANES, 2 * LANES), 1)
    sub = lax.broadcasted_iota(jnp.int32, (2 * LANES, 2 * LANES), 0)
    seg = (lane // RW_HEAD == sub // RW_HEAD).astype(BF16)

    kkv = k * kk_ref[...]
    kkn = kkv * lax.rsqrt(jnp.maximum(_head_sums(kkv * kkv, seg), 1e-24))
    k2 = k * (1.0 + (a - 1.0) * ka_ref[...])
    aa = -kkn
    bb = kkn * a
    bonus = _head_sums(r * k2 * rk_ref[...], seg)

    tr = lax.broadcasted_iota(jnp.int32, (N, N), 0)
    tc = lax.broadcasted_iota(jnp.int32, (N, N), 1)
    tri = jnp.logical_and(tr // C == tc // C, tr >= tc).astype(BF16)
    cl = _dot_hilo(tri, lw)
    cl3 = cl.reshape(nb, C, W)
    cl_end3 = cl3[:, C - 1:C, :]
    p_in = jnp.exp(cl)
    p_ex = jnp.exp(cl - lw)
    p_inv = jnp.exp(-cl)
    p_end = jnp.exp(cl_end3 - cl3).reshape(N, W)
    p_tot3 = jnp.exp(cl_end3)

    SW = hps // 2 * LANES
    lane_c = lax.broadcasted_iota(jnp.int32, (1, C, SW), 2)
    m1 = (lane_c % LANES < RW_HEAD).astype(BF16)
    m2 = (lane_c % LANES >= RW_HEAD).astype(BF16)

    def slab(x, i):
        b, p = divmod(i, RW_PAIRS)
        return x[b * C:(b + 1) * C, p * LANES:(p + 1) * LANES]

    def pairs(x):
        if packed:
            return jnp.stack([jnp.concatenate([slab(x, g), slab(x, g + G)], axis=1) for g in range(G)], axis=0)
        return jnp.stack([slab(x, g) for g in range(G)], axis=0)

    def stack(x):
        xp = pairs(x).astype(BF16)
        return jnp.concatenate([xp * m1, xp * m2], axis=1)

    def widen(mask):
        return jnp.concatenate([mask] * (hps // 2), axis=1)

    rr = lax.broadcasted_iota(jnp.int32, (2 * C, 2 * C), 0)
    cc = lax.broadcasted_iota(jnp.int32, (2 * C, 2 * C), 1)
    same = (rr // C) == (cc // C)
    strict = widen(jnp.logical_and(same, (rr % C) > (cc % C)))
    incl = widen(jnp.logical_and(same, (rr % C) >= (cc % C)))
    base = 8
    eye = widen((rr == cc).astype(F32))
    base_mask = widen(jnp.logical_and(rr // base == cc // base, rr > cc))
    merge_masks = []
    kb = base
    while kb < C:
        merge_masks.append(widen(jnp.logical_and(rr // (2 * kb) == cc // (2 * kb), rr // kb == cc // kb + 1)))
        kb *= 2

    def bdiag(x):
        lo, hi = x[..., :LANES], x[..., LANES:]
        z = jnp.zeros_like(lo)
        return jnp.concatenate([jnp.concatenate([lo, z], axis=2), jnp.concatenate([z, hi], axis=2)], axis=1)

    def mm(a_, b_):
        return _bmm(a_, bdiag(b_.astype(BF16))) if packed else _bmm(a_, b_)

    s_bd = sbd_scr[...]
    ar = jnp.concatenate([stack(aa * p_ex), stack(r * p_in)], axis=1)
    bk = jnp.concatenate([stack(bb * p_inv), stack(k2 * p_inv)], axis=1)
    vm = stack(v)
    mats = [_bmm_nt(ar[..., h * LANES:(h + 1) * LANES], bk[..., h * LANES:(h + 1) * LANES])
            for h in range(hps // 2)]
    blk = lambda i, j: jnp.concatenate([mt[:, i * 2 * C:(i + 1) * 2 * C, j * 2 * C:(j + 1) * 2 * C] for mt in mats],
                                       axis=2)
    l_ab = jnp.where(strict, blk(0, 0), 0.0)
    a_ak = jnp.where(strict, blk(0, 1), 0.0)
    a_rb = jnp.where(incl, blk(1, 0), 0.0)
    a_rk = jnp.where(incl, blk(1, 1), 0.0)
    gs = _bmm_nt(ar, s_bd)
    inv = _tril_inverse(l_ab, eye, base_mask, merge_masks, mm)
    x = mm(inv, gs[:, :2 * C] + mm(a_ak, vm))
    ym = gs[:, 2 * C:] + mm(a_rb, x) + mm(a_rk, vm)
    yp = ym[:, :C] + ym[:, C:]
    bkh = jnp.concatenate([stack(bb * p_end), stack(k2 * p_end)], axis=1)
    uv = jnp.concatenate([x, vm.astype(F32)], axis=1)
    upd = _bmm(jnp.stack([uv[i].T for i in range(G)], axis=0), bkh)
    if packed:
        z = jnp.zeros((G, LANES, LANES), F32)
        upd = jnp.concatenate([jnp.concatenate([upd[:, :LANES, :LANES], z], axis=2),
                               jnp.concatenate([z, upd[:, LANES:, LANES:]], axis=2)], axis=1)
    tot = lambda i: p_tot3[i // RW_PAIRS, :, (i % RW_PAIRS) * LANES:(i % RW_PAIRS + 1) * LANES]
    p_tot = jnp.stack([jnp.concatenate([tot(g), tot(g + G)], axis=1) if packed else tot(g) for g in range(G)], axis=0)
    s_new = s_bd * p_tot + upd
    sbd_scr[...] = s_new

    def y_slab(i):
        return yp[i % G][:, (i // G) * LANES:(i // G + 1) * LANES]

    y = jnp.concatenate(
        [jnp.concatenate([y_slab(b * RW_PAIRS + p) for p in range(RW_PAIRS)], axis=1) for b in range(nb)], axis=0)

    inv_n = 1.0 / RW_HEAD
    mean = _head_sums(y, seg) * inv_n
    d = y - mean
    var = _head_sums(d * d, seg) * inv_n
    yn = d * lax.rsqrt(var + LNX_EPS) * lnw_ref[...] + lnb_ref[...]
    y_ref[...] = ((yn + bonus * v) * g).reshape(nb, C, W).astype(y_ref.dtype)

    @pl.when(c_idx == n_chunks - 1)
    def _():
        last = (seq - 1) % C
        shout_ref[...] = zr3[:, last:last + 1, :]
        for g in range(G):
            for n_, (b, h) in enumerate(slot_heads(g)):
                sout_ref[b, h] = s_new[g][n_ * RW_HEAD:(n_ + 1) * RW_HEAD, n_ * RW_HEAD:(n_ + 1) * RW_HEAD]


def _rwkv(zr, shift0, wkv0, v_first, prm, *, chunk, seq, nb, layer0):
    b, lpad, cols = zr.shape
    assert b % nb == 0
    n_chunks = pl.cdiv(lpad, chunk)
    W = RW_WIDTH
    row = lambda n: pl.BlockSpec((1, n), lambda bi, c: (0, 0))
    full = lambda s: pl.BlockSpec(s, lambda bi, c: (0, 0))
    tok = lambda n: pl.BlockSpec((nb, chunk, n), lambda bi, c: (bi, c, 0))
    state = pl.BlockSpec((nb, RW_HEADS, RW_HEAD, RW_HEAD), lambda bi, c: (bi, 0, 0, 0))
    shift = pl.BlockSpec((nb, 1, cols), lambda bi, c: (bi, 0, 0))
    in_specs = [tok(cols), shift, state]
    args = [zr, shift0.reshape(b, 1, cols), wkv0]
    if not layer0:
        in_specs.append(tok(W))
        args.append(v_first)
    in_specs += [row(cols), row(W), full((LANES, W)), row(W), full((LANES, W)), full((GATE_LORA, W))]
    args += [prm['mu'], prm['w0'], prm['w2'], prm['a0'], prm['a2'], prm['g2']]
    if not layer0:
        in_specs += [row(W), full((W, LANES)), full((LANES, W))]
        args += [prm['v0'], prm['v1'], prm['v2']]
    in_specs += [row(W)] * 5
    args += [prm['k_k'], prm['k_a'], prm['r_k'], prm['ln_w'], prm['ln_b']]
    out_shape = [jax.ShapeDtypeStruct((b, lpad, W), BF16),
                 jax.ShapeDtypeStruct((b, RW_HEADS, RW_HEAD, RW_HEAD), F32),
                 jax.ShapeDtypeStruct((b, 1, cols), F32)]
    out_specs = [tok(W), state, shift]
    if layer0:
        out_shape.append(jax.ShapeDtypeStruct((b, lpad, W), F32))
        out_specs.append(tok(W))
    outs = pl.pallas_call(
        functools.partial(_rwkv_kernel, chunk=chunk, seq=seq, layer0=layer0),
        grid=(b // nb, n_chunks),
        in_specs=in_specs,
        out_specs=out_specs,
        out_shape=out_shape,
        scratch_shapes=[pltpu.VMEM((nb * RW_PAIRS // 2, 2 * LANES, 2 * LANES) if 2 * chunk == LANES
                                   else (nb * RW_PAIRS, LANES, LANES), F32),
                        pltpu.VMEM((nb, chunk + SUBLANES, cols), F32)],
        compiler_params=_params(("parallel", "arbitrary")),
        name="rwkv",
    )(*args)
    if layer0:
        y, s_out, sh_out, vf = outs
    else:
        (y, s_out, sh_out), vf = outs, v_first
    return y, s_out, sh_out.reshape(b, cols), vf


def _rms(x, g):
    ms = jnp.mean(x * x, axis=-1, keepdims=True)
    return x * lax.rsqrt(ms + 1e-6) * g


def _merge_kernel(att_ref, rw_ref, ga_ref, gb_ref, x_ref, wa_ref, wb_ref, wo_ref, n2_ref, o_ref):
    ya = _bdot(att_ref[...], wa_ref[...])
    yb = _bdot(rw_ref[...], wb_ref[...])
    merged = _sigmoid(ga_ref[...].astype(F32)) * ya + _sigmoid(gb_ref[...].astype(F32)) * yb
    mix = _bdot(merged, wo_ref[...])
    o_ref[...] = x_ref[...] + _rms(mix, n2_ref[...])


def _merge(att, rw, gates, x, wa, wb, wo, n2, *, tm):
    t, d = x.shape
    assert t % tm == 0
    tokb = lambda j: pl.BlockSpec((tm, d), lambda i: (i, j))
    wspec = pl.BlockSpec((d, d), lambda i: (0, 0))
    return pl.pallas_call(
        _merge_kernel,
        grid=(t // tm,),
        in_specs=[tokb(0), tokb(0), tokb(0), tokb(1), tokb(0), wspec, wspec, wspec,
                  pl.BlockSpec((1, d), lambda i: (0, 0))],
        out_specs=tokb(0),
        out_shape=jax.ShapeDtypeStruct((t, d), F32),
        compiler_params=_params(("parallel",)),
        name="merge",
    )(att, rw, gates, gates, x, wa, wb, wo, n2.reshape(1, d))


def _ffn_kernel(x_ref, n3_ref, wu_ref, wd_ref, n4_ref, o_ref, *, n_split):
    x = x_ref[...]
    h = _rms(x, n3_ref[...]).astype(BF16)
    dff = wu_ref.shape[1]
    step = dff // n_split
    f = jnp.zeros(x.shape, F32)
    for s in range(n_split):
        u = jnp.dot(h, wu_ref[:, s * step:(s + 1) * step], preferred_element_type=F32)
        u = jnp.square(jnp.maximum(u, 0.0))
        f = f + jnp.dot(u.astype(BF16), wd_ref[s * step:(s + 1) * step, :], preferred_element_type=F32)
    o_ref[...] = x + _rms(f, n4_ref[...])


def _ffn(x, n3, wu, wd, n4, *, tm):
    t, d = x.shape
    dff = wu.shape[1]
    assert t % tm == 0
    tokb = pl.BlockSpec((tm, d), lambda i: (i, 0))
    vec = pl.BlockSpec((1, d), lambda i: (0, 0))
    return pl.pallas_call(
        functools.partial(_ffn_kernel, n_split=4),
        grid=(t // tm,),
        in_specs=[tokb, vec, pl.BlockSpec((d, dff), lambda i: (0, 0)), pl.BlockSpec((dff, d), lambda i: (0, 0)), vec],
        out_specs=tokb,
        out_shape=jax.ShapeDtypeStruct((t, d), F32),
        compiler_params=_params(("parallel",)),
        name="ffn",
    )(x, n3.reshape(1, d), wu, wd, n4.reshape(1, d))


def _pick_tile(n, candidates):
    for c in candidates:
        if n % c == 0:
            return c
    return n


def kernel(x_prompt, x_sample, cache_k, cache_v, state_wkv, state_shift, page_table, meta_tokens, norm_pre_mix, norm_post_mix, norm_pre_ffn, norm_post_ffn, w_in, lambda_q1, lambda_k1, lambda_q2, lambda_k2, da_subln, rw_mu, rw_w0, rw_w2, rw_a0, rw_a2, rw_g2, rw_v0, rw_v1, rw_v2, rw_k_k, rw_k_a, rw_r_k, rw_ln_w, rw_ln_b, w_branch_a, w_branch_b, w_out, w_up, w_down):
    depth = w_in.shape[0]
    b, seq_in, d = x_prompt.shape
    db, ds, _ = x_sample.shape
    seq = seq_in + N_META
    n_past = page_table.shape[1] * PAGE_SIZE
    qk_w = DA_MAPS * DA_HEAD_DIM
    da_w = DA_HEADS * DA_VALUE_DIM
    assert ds <= SUBLANES

    xp = jnp.concatenate([jnp.broadcast_to(meta_tokens[None], (b, N_META, d)), x_prompt], axis=1)
    xs = x_sample.reshape(1, db * ds, d)
    tabs_p = _rope_tables(jnp.arange(seq))
    tabs_s = _rope_tables(n_past + jnp.arange(db * ds) % ds)
    tm_p = _pick_tile(seq, (688, 344, 72, 48, 16))
    tm_s = _pick_tile(db * ds, (512, 256, 128, 64, 32, 16, 8))
    tm_flat_p = _pick_tile(b * seq, (384, 256, 128, 64, 48, 32, 16, 8))
    attn_tq = 128
    chunk_p = 64
    chunk_s = SUBLANES
    nb_s = _pick_tile(db, (8, 4, 2, 1))
    nb_p = _pick_tile(b, (2, 1))

    cache_kt = cache_k.transpose(0, 1, 3, 4, 2)
    cache_v2 = cache_v.reshape(depth, cache_v.shape[1], PAGE_SIZE * DA_HEADS, DA_VALUE_DIM)
    w_in_b = w_in.astype(BF16)
    zeros_lora = jnp.zeros((DECAY_LORA, RW_WIDTH), BF16)
    wkv0_p = jnp.zeros((b, RW_HEADS, RW_HEAD, RW_HEAD), F32)
    shift0_p = jnp.zeros((b, RW_COLS), F32)
    q_scale = DA_HEAD_DIM ** -0.5 * LOG2E
    pad_tok = lambda t: jnp.pad(t, ((0, 0), (0, SUBLANES - ds)) + ((0, 0),) * (t.ndim - 2))

    vf_p = vf_s = None
    outs = [[] for _ in range(8)]
    for l in range(depth):
        o = 0
        wq = w_in_b[l, :, o:o + qk_w]; o += qk_w
        wk = w_in_b[l, :, o:o + qk_w]; o += qk_w
        wv = w_in_b[l, :, o:o + da_w]; o += da_w
        wz = w_in_b[l, :, o:o + RW_COLS]; o += RW_COLS
        wg = w_in_b[l, :, o:o + 2 * d]
        row = lambda t: t.reshape(1, -1).astype(F32)
        prm = {'mu': row(rw_mu[l]), 'w0': row(rw_w0[l]), 'a0': row(rw_a0[l]),
               'w2': jnp.concatenate([rw_w2[l].astype(BF16), zeros_lora], axis=0),
               'a2': jnp.concatenate([zeros_lora, rw_a2[l].astype(BF16)], axis=0),
               'g2': rw_g2[l].astype(BF16),
               'k_k': row(rw_k_k[l]), 'k_a': row(rw_k_a[l]), 'r_k': row(rw_r_k[l]),
               'ln_w': row(rw_ln_w[l]), 'ln_b': row(rw_ln_b[l])}
        if l > 0:
            prm['v0'] = row(rw_v0[l - 1])
            prm['v1'] = jnp.pad(rw_v1[l - 1].astype(BF16), ((0, 0), (0, LANES - MV_LORA)))
            prm['v2'] = jnp.pad(rw_v2[l - 1].astype(BF16), ((0, LANES - MV_LORA), (0, 0)))
        lam_init = 0.8 - 0.6 * math.exp(-0.3 * l)
        lam_vecs = [t[l].reshape(1, DA_HEAD_DIM).astype(F32) for t in (lambda_q1, lambda_k1, lambda_q2, lambda_k2)]
        wa, wb, wo = w_branch_a[l].astype(BF16), w_branch_b[l].astype(BF16), w_out[l].astype(BF16)
        wu, wd = w_up[l].astype(BF16), w_down[l].astype(BF16)
        n1 = norm_pre_mix[l]

        def project(x3, tm, tabs, q_dtype):
            q = _norm_matmul(x3, n1, wq, tm=tm, tn=qk_w, out_dtype=q_dtype, rope_tabs=tabs, scale=q_scale)
            k = _norm_matmul(x3, n1, wk, tm=tm, tn=qk_w, out_dtype=F32, rope_tabs=tabs)
            v = _norm_matmul(x3, n1, wv, tm=tm, tn=da_w, out_dtype=F32)
            zr = _norm_matmul(x3, n1, wz, tm=tm, tn=RW_COLS // 2, out_dtype=F32)
            gt = _norm_matmul(x3, n1, wg, tm=tm, tn=d, out_dtype=BF16)
            return q, k, v, zr, gt

        q, k, v, zr, gt = project(xp, tm_p, tabs_p, BF16)
        att = _attn_prompt(q, k, v, lam_vecs, da_subln[l], lam_init, tq=attn_tq)
        rw, wkv_p, sh_p, vf_p = _rwkv(zr, shift0_p, wkv0_p, vf_p, prm, chunk=chunk_p, seq=seq, nb=nb_p,
                                      layer0=(l == 0))
        flat = lambda t: t.reshape(b * seq, -1)
        x2 = _merge(flat(att), flat(rw), flat(gt), flat(xp), wa, wb, wo, norm_post_mix[l], tm=tm_flat_p)
        x2 = _ffn(x2, norm_pre_ffn[l], wu, wd, norm_post_ffn[l], tm=tm_flat_p)
        xp = x2.reshape(b, seq, d)
        outs[0].append(k.reshape(b, seq, DA_MAPS, DA_HEAD_DIM))
        outs[1].append(v.reshape(b, seq, DA_HEADS, DA_VALUE_DIM))
        outs[2].append(wkv_p)
        outs[3].append(sh_p)

        q, k, v, zr, gt = project(xs, tm_s, tabs_s, F32)
        k4 = k.reshape(db, ds, DA_MAPS, DA_HEAD_DIM)
        v4 = v.reshape(db, ds, DA_HEADS, DA_VALUE_DIM)
        q4 = pad_tok(q.reshape(db, ds, DA_MAPS, DA_HEAD_DIM)).transpose(0, 2, 1, 3)
        att8 = _attn_sample(l, q4, cache_kt, cache_v2, page_table, pad_tok(k4), pad_tok(v4), lam_vecs,
                            da_subln[l], lam_init, ds)
        att = att8[:, :ds].reshape(db * ds, da_w)
        rw8, wkv_s, sh_s, vf_s = _rwkv(pad_tok(zr.reshape(db, ds, RW_COLS)), state_shift[l], state_wkv[l], vf_s,
                                        prm, chunk=chunk_s, seq=ds, nb=nb_s, layer0=(l == 0))
        rw = rw8[:, :ds].reshape(db * ds, RW_WIDTH)
        x2 = _merge(att, rw, gt.reshape(db * ds, -1), xs.reshape(db * ds, d), wa, wb, wo, norm_post_mix[l], tm=tm_s)
        x2 = _ffn(x2, norm_pre_ffn[l], wu, wd, norm_post_ffn[l], tm=tm_s)
        xs = x2.reshape(1, db * ds, d)
        outs[4].append(k4)
        outs[5].append(v4)
        outs[6].append(wkv_s)
        outs[7].append(sh_s)

    y_prompt = xp[:, N_META:]
    y_sample = xs.reshape(db, ds, d)
    return (y_prompt, y_sample) + tuple(jnp.stack(o) for o in outs)
```

```python
import functools
import math

import jax
import jax.numpy as jnp
from jax import lax
from jax.experimental import pallas as pl
from jax.experimental.pallas import tpu as pltpu

F32 = jnp.float32
BF16 = jnp.bfloat16

D_MODEL = 1024
N_META = 16
DA_HEADS = 8
DA_MAPS = 2 * DA_HEADS
DA_HEAD_DIM = 64
DA_VALUE_DIM = 128
ROPE_THETA = 10000.0
SUBLN_EPS = 1e-5
RW_HEAD = 64
RW_HEADS = 16
RW_PAIRS = RW_HEADS // 2
RW_WIDTH = 1024
DECAY_LORA = 64
AAA_LORA = 64
GATE_LORA = 128
MV_LORA = 32
RW_COLS = 3 * RW_WIDTH + DECAY_LORA + AAA_LORA + GATE_LORA
LNX_EPS = 64e-5
PAGE_SIZE = 128
LANES = 128
SUBLANES = 8
NEG = -1e30
LOG2E = 1.4426950408889634
VMEM_LIMIT = 52 * 1024 * 1024

NT_DIMS = (((1,), (1,)), ((), ()))


def _bdot(a, b):
    return jnp.dot(a.astype(BF16), b.astype(BF16), preferred_element_type=F32)


def _bmm(a, b):
    return jnp.einsum('pij,pjk->pik', a.astype(BF16), b.astype(BF16), preferred_element_type=F32)


def _bmm_nt(a, b):
    return jnp.einsum('pik,pjk->pij', a.astype(BF16), b.astype(BF16), preferred_element_type=F32)


def _sigmoid(x):
    return 1.0 / (1.0 + jnp.exp(-x))


def _params(sem):
    return pltpu.CompilerParams(dimension_semantics=sem, vmem_limit_bytes=VMEM_LIMIT)


def _norm_matmul_kernel(x_ref, g_ref, *refs, ropes, scales):
    n_out = len(ropes)
    w_refs, rest = refs[:n_out], refs[n_out:]
    if any(ropes):
        cos, sa, sb = (r[...] for r in rest[:3])
        rest = rest[3:]
    x = x_ref[0]
    ms = jnp.mean(x * x, axis=-1, keepdims=True)
    h = (x * lax.rsqrt(ms + 1e-6) * g_ref[...]).astype(BF16)
    for w_ref, o_ref, rope, scale in zip(w_refs, rest, ropes, scales):
        acc = jnp.dot(h, w_ref[...], preferred_element_type=F32)
        if rope:
            for c in range(acc.shape[1] // LANES):
                blk = acc[:, c * LANES:(c + 1) * LANES]
                out = blk * cos + pltpu.roll(blk, 96, 1) * sa + pltpu.roll(blk, 32, 1) * sb
                if scale != 1.0:
                    out = out * scale
                o_ref[0, :, c * LANES:(c + 1) * LANES] = out.astype(o_ref.dtype)
        else:
            o_ref[0] = acc.astype(o_ref.dtype)


def _norm_matmul(x3, g, ws, *, tm, out_dtypes, ropes, scales, rope_tabs):
    bv, lv, d = x3.shape
    assert lv % tm == 0
    in_specs = [pl.BlockSpec((1, tm, d), lambda b, i: (b, i, 0)), pl.BlockSpec((1, d), lambda b, i: (0, 0))]
    in_specs += [pl.BlockSpec(w.shape, lambda b, i: (0, 0)) for w in ws]
    args = [x3, g.reshape(1, d), *ws]
    if any(ropes):
        in_specs += [pl.BlockSpec((tm, LANES), lambda b, i: (i, 0))] * 3
        args += list(rope_tabs)
    return pl.pallas_call(
        functools.partial(_norm_matmul_kernel, ropes=tuple(ropes), scales=tuple(scales)),
        grid=(bv, lv // tm),
        in_specs=in_specs,
        out_specs=[pl.BlockSpec((1, tm, w.shape[1]), lambda b, i: (b, i, 0)) for w in ws],
        out_shape=[jax.ShapeDtypeStruct((bv, lv, w.shape[1]), dt) for w, dt in zip(ws, out_dtypes)],
        compiler_params=_params(("parallel", "parallel")),
        name="norm_matmul",
    )(*args)


def _rope_tables(pos):
    half = DA_HEAD_DIM // 2
    inv_freq = ROPE_THETA ** (-jnp.arange(half, dtype=F32) * (2.0 / DA_HEAD_DIM))
    ang = pos.astype(F32)[:, None] * inv_freq[None, :]
    cos, sin = jnp.cos(ang), jnp.sin(ang)
    zero = jnp.zeros_like(sin)
    cos_t = jnp.concatenate([cos] * 4, axis=1)
    sa_t = jnp.concatenate([-sin, zero] * 2, axis=1)
    sb_t = jnp.concatenate([zero, sin] * 2, axis=1)
    return cos_t, sa_t, sb_t


def _lambda(lq1, lk1, lq2, lk2, lam_init):
    s1 = jnp.sum(lq1[...] * lk1[...], axis=-1, keepdims=True)
    s2 = jnp.sum(lq2[...] * lk2[...], axis=-1, keepdims=True)
    return jnp.exp(s1) - jnp.exp(s2) + lam_init


def _attn_prompt_kernel(q_ref, k_ref, v_ref, lq1, lk1, lq2, lk2, sub_ref, o_ref, kb_scr, vb_scr, s_scr, p_scr,
                        *, seq, tq, lam_init):
    lam = _lambda(lq1, lk1, lq2, lk2, lam_init)
    sub = sub_ref[...] * (1.0 - lam_init)
    n_tiles = pl.cdiv(seq, tq)
    pad_rows = kb_scr.shape[0] - seq
    kb_scr[0:seq, :] = k_ref[0].astype(BF16)
    vb_scr[0:seq, :] = v_ref[0].astype(BF16)
    if pad_rows:
        kb_scr[seq:seq + pad_rows, :] = jnp.zeros((pad_rows, LANES), BF16)
        vb_scr[seq:seq + pad_rows, :] = jnp.zeros((pad_rows, LANES), BF16)

    for i in range(n_tiles):
        row0 = i * tq
        rows = min(tq, seq - row0)
        nk = row0 + tq
        q = q_ref[0, row0:row0 + rows, :]
        lo = lax.broadcasted_iota(jnp.int32, (rows, LANES), 1) < DA_HEAD_DIM
        zero = jnp.zeros_like(q)
        q2 = jnp.concatenate([jnp.where(lo, q, zero), jnp.where(lo, zero, q)], axis=0)
        s = lax.dot_general(q2, kb_scr[0:nk, :], NT_DIMS, preferred_element_type=F32)
        qpos = lax.broadcasted_iota(jnp.int32, (2 * rows, tq), 0) % rows
        kpos = lax.broadcasted_iota(jnp.int32, (2 * rows, tq), 1)
        n_blk = nk // LANES
        blocks = [s[:, b * LANES:(b + 1) * LANES] for b in range(n_blk)]
        for b in range(row0 // LANES, n_blk):
            off = b * LANES - row0
            blocks[b] = jnp.where(kpos[:, off:off + LANES] <= qpos[:, off:off + LANES], blocks[b], NEG)
        m_acc = functools.reduce(jnp.maximum, blocks)
        for b in range(n_blk):
            s_scr[0:2 * rows, b * LANES:(b + 1) * LANES] = blocks[b]
        m = jnp.max(m_acc, axis=-1, keepdims=True)
        l_acc = jnp.zeros((2 * rows, LANES), F32)
        for b in range(n_blk):
            p = jnp.exp2(s_scr[0:2 * rows, b * LANES:(b + 1) * LANES] - m)
            l_acc = l_acc + p
            p_scr[0:2 * rows, b * LANES:(b + 1) * LANES] = p.astype(BF16)
        l = jnp.sum(l_acc, axis=-1, keepdims=True)
        acc = jnp.dot(p_scr[0:2 * rows, 0:nk], vb_scr[0:nk, :], preferred_element_type=F32) / l
        o = acc[0:rows] - lam * acc[rows:2 * rows]
        ms = jnp.mean(o * o, axis=-1, keepdims=True)
        o_ref[0, row0:row0 + rows, :] = (o * lax.rsqrt(ms + SUBLN_EPS) * sub).astype(o_ref.dtype)


def _attn_prompt(q, k, v, lam_vecs, subln, lam_init, *, tq):
    b, seq, _ = q.shape
    npad = pl.cdiv(seq, tq) * tq
    vec = pl.BlockSpec((1, DA_HEAD_DIM), lambda bi, h: (0, 0))
    blk = pl.BlockSpec((1, seq, LANES), lambda bi, h: (bi, 0, h))
    return pl.pallas_call(
        functools.partial(_attn_prompt_kernel, seq=seq, tq=tq, lam_init=lam_init),
        grid=(b, DA_HEADS),
        in_specs=[blk, blk, blk, vec, vec, vec, vec, pl.BlockSpec((1, LANES), lambda bi, h: (0, 0))],
        out_specs=blk,
        out_shape=jax.ShapeDtypeStruct((b, seq, DA_HEADS * DA_VALUE_DIM), BF16),
        scratch_shapes=[pltpu.VMEM((npad, LANES), BF16), pltpu.VMEM((npad, LANES), BF16),
                        pltpu.VMEM((2 * tq, npad), F32), pltpu.VMEM((2 * tq, npad), BF16)],
        compiler_params=_params(("parallel", "parallel")),
        name="attn_prompt",
    )(q, k, v, *lam_vecs, subln.reshape(1, LANES))


def _attn_sample_kernel(pt_ref, q_ref, *refs, n_steps, pp, n_new, lam_init):
    del pt_ref
    k_refs, v_refs = refs[:pp], refs[pp:2 * pp]
    kn_ref, vn_ref, lq1, lk1, lq2, lk2, sub_ref, o_ref, m_scr, l_scr, acc_scr = refs[2 * pp:]
    p = pl.program_id(1)
    rpm = SUBLANES
    rows = DA_MAPS * rpm

    @pl.when(p == 0)
    def _():
        m_scr[...] = jnp.full(m_scr.shape, NEG, F32)
        l_scr[...] = jnp.zeros(l_scr.shape, F32)
        acc_scr[...] = jnp.zeros(acc_scr.shape, F32)

    def update(scores, get_v, keep):
        s = jnp.concatenate([scores(q_ref[0, c].astype(BF16), c) for c in range(DA_MAPS)], axis=0)
        if keep is not None:
            s = jnp.where(keep, s, NEG)
        m_old = m_scr[...]
        mn = jnp.maximum(m_old, jnp.max(s, axis=-1, keepdims=True))
        al = jnp.exp2(m_old - mn)
        pr = jnp.exp2(s - mn)
        l_scr[...] = al * l_scr[...] + jnp.sum(pr, axis=-1, keepdims=True)
        m_scr[...] = mn
        pb = pr.astype(BF16)
        pv = jnp.concatenate(
            [jnp.dot(pb[2 * rpm * h:2 * rpm * (h + 1)], get_v(h), preferred_element_type=F32)
             for h in range(DA_HEADS)], axis=0)
        acc_scr[...] = al * acc_scr[...] + pv

    update(lambda qc, c: jnp.concatenate(
               [jnp.dot(qc, k_ref[c].astype(BF16), preferred_element_type=F32) for k_ref in k_refs], axis=1),
           lambda h: jnp.concatenate(
               [v_ref[pl.ds(h, PAGE_SIZE, stride=DA_HEADS), :].astype(BF16) for v_ref in v_refs], axis=0),
           None)

    @pl.when(p == n_steps - 1)
    def _():
        n_fill = PAGE_SIZE - kn_ref.shape[1]
        kfill = jnp.zeros((n_fill, DA_HEAD_DIM), F32)
        vfill = jnp.zeros((n_fill, DA_VALUE_DIM), F32)
        r = lax.broadcasted_iota(jnp.int32, (rows, PAGE_SIZE), 0) % rpm
        c = lax.broadcasted_iota(jnp.int32, (rows, PAGE_SIZE), 1)
        keep = jnp.logical_and(c <= r, c < n_new)
        update(lambda qc, ci: lax.dot_general(
                   qc, jnp.concatenate([kn_ref[0, :, ci, :], kfill], axis=0).astype(BF16), NT_DIMS,
                   preferred_element_type=F32),
               lambda h: jnp.concatenate([vn_ref[0, :, h, :], vfill], axis=0).astype(BF16), keep)
        lam = _lambda(lq1, lk1, lq2, lk2, lam_init)
        sub = sub_ref[...] * (1.0 - lam_init)
        y = acc_scr[...] / l_scr[...]
        for h in range(DA_HEADS):
            r0 = 2 * rpm * h
            o = y[r0:r0 + rpm] - lam * y[r0 + rpm:r0 + 2 * rpm]
            ms = jnp.mean(o * o, axis=-1, keepdims=True)
            o_ref[0, :, h * LANES:(h + 1) * LANES] = o * lax.rsqrt(ms + SUBLN_EPS) * sub


def _attn_sample(layer, q, cache_kt, cache_v2, page_table, k_new, v_new, lam_vecs, subln, lam_init, n_new):
    nb = q.shape[0]
    n_pages = page_table.shape[1]
    pp = _pick_tile(n_pages, (8, 4, 2, 1))
    n_steps = n_pages // pp
    rows = DA_MAPS * SUBLANES
    vec = pl.BlockSpec((1, DA_HEAD_DIM), lambda b, p, pt: (0, 0))
    seq4 = lambda s: pl.BlockSpec((1,) + s, lambda b, p, pt: (b, 0, 0, 0))

    def page_of(j):
        return lambda b, p, pt: pt[b, p * pp + j]

    k_specs = [pl.BlockSpec((None, None, DA_MAPS, DA_HEAD_DIM, PAGE_SIZE),
                            lambda b, p, pt, f=page_of(j): (layer, f(b, p, pt), 0, 0, 0)) for j in range(pp)]
    v_specs = [pl.BlockSpec((None, None, PAGE_SIZE * DA_HEADS, DA_VALUE_DIM),
                            lambda b, p, pt, f=page_of(j): (layer, f(b, p, pt), 0, 0)) for j in range(pp)]
    grid_spec = pltpu.PrefetchScalarGridSpec(
        num_scalar_prefetch=1,
        grid=(nb, n_steps),
        in_specs=[seq4((DA_MAPS, SUBLANES, DA_HEAD_DIM))] + k_specs + v_specs + [
                  seq4((SUBLANES, DA_MAPS, DA_HEAD_DIM)), seq4((SUBLANES, DA_HEADS, DA_VALUE_DIM)),
                  vec, vec, vec, vec, pl.BlockSpec((1, LANES), lambda b, p, pt: (0, 0))],
        out_specs=pl.BlockSpec((1, SUBLANES, DA_HEADS * DA_VALUE_DIM), lambda b, p, pt: (b, 0, 0)),
        scratch_shapes=[pltpu.VMEM((rows, 1), F32), pltpu.VMEM((rows, 1), F32), pltpu.VMEM((rows, LANES), F32)],
    )
    return pl.pallas_call(
        functools.partial(_attn_sample_kernel, n_steps=n_steps, pp=pp, n_new=n_new, lam_init=lam_init),
        grid_spec=grid_spec,
        out_shape=jax.ShapeDtypeStruct((nb, SUBLANES, DA_HEADS * DA_VALUE_DIM), F32),
        compiler_params=_params(("parallel", "arbitrary")),
        name="attn_sample",
    )(page_table, q, *([cache_kt] * pp), *([cache_v2] * pp), k_new, v_new, *lam_vecs, subln.reshape(1, LANES))


def _split_bf16(x):
    hi = x.astype(BF16)
    lo = (x - hi.astype(F32)).astype(BF16)
    return hi, lo


def _dot_hilo(a, x):
    hi, lo = _split_bf16(x)
    return jnp.dot(a, hi, preferred_element_type=F32) + jnp.dot(a, lo, preferred_element_type=F32)


def _head_sums(x, seg):
    n = x.shape[0]
    w = seg.shape[0]
    n_p = x.shape[1] // w
    xs = jnp.concatenate([x[:, p * w:(p + 1) * w] for p in range(n_p)], axis=0)
    out = jnp.dot(xs.astype(BF16), seg, preferred_element_type=F32)
    return jnp.concatenate([out[p * n:(p + 1) * n] for p in range(n_p)], axis=1)


def _tril_inverse(l_ab, eye, base_mask, merge_masks, mm):
    l8 = jnp.where(base_mask, l_ab, 0.0)
    inv = eye + l8
    lp = l8
    for _ in range(2):
        lp = mm(lp, lp)
        inv = inv + mm(lp, inv)
    for em in merge_masks:
        inv = inv + mm(inv, mm(jnp.where(em, l_ab, 0.0), inv))
    return inv


def _rwkv_kernel(*refs, chunk, seq, layer0):
    if layer0:
        (zr_ref, sh0_ref, s0_ref, mu_ref, w0_ref, w2_ref, a0_ref, a2_ref, g2_ref,
         kk_ref, ka_ref, rk_ref, lnw_ref, lnb_ref,
         y_ref, sout_ref, shout_ref, vf_out_ref, sbd_scr, buf_scr) = refs
    else:
        (zr_ref, sh0_ref, s0_ref, vf_ref, mu_ref, w0_ref, w2_ref, a0_ref, a2_ref, g2_ref,
         v0_ref, v1_ref, v2_ref, kk_ref, ka_ref, rk_ref, lnw_ref, lnb_ref,
         y_ref, sout_ref, shout_ref, sbd_scr, buf_scr) = refs
    C = chunk
    W = RW_WIDTH
    nb = zr_ref.shape[0]
    N = nb * C
    P = nb * RW_PAIRS
    cols = zr_ref.shape[2]
    c_idx = pl.program_id(1)
    n_chunks = pl.num_programs(1)
    packed = 2 * C == LANES and P % 2 == 0
    hps = 4 if packed else 2
    G = P // 2 if packed else P
    zero64 = jnp.zeros((RW_HEAD, RW_HEAD), F32)

    def slot_heads(g):
        prs = (g, g + G) if packed else (g,)
        return [(i // RW_PAIRS, 2 * (i % RW_PAIRS) + h) for i in prs for h in range(2)]

    @pl.when(c_idx == 0)
    def _():
        for g in range(G):
            rows_ = []
            for n_, (b, h) in enumerate(slot_heads(g)):
                rows_.append(jnp.concatenate(
                    [s0_ref[b, h] if j == n_ else zero64 for j in range(hps)], axis=1))
            sbd_scr[g] = jnp.concatenate(rows_, axis=0)
        buf_scr[:, SUBLANES - 1:SUBLANES, :] = sh0_ref[...]

    zr3 = zr_ref[...]
    tpos = c_idx * C + lax.broadcasted_iota(jnp.int32, (nb, C, 1), 1)
    valid3 = tpos < seq
    zr3 = jnp.where(valid3, zr3, 0.0)
    buf_scr[:, SUBLANES:SUBLANES + C, :] = zr3
    prev3 = buf_scr[:, SUBLANES - 1:SUBLANES - 1 + C, :]
    buf_scr[:, SUBLANES - 1:SUBLANES, :] = zr3[:, C - 1:C, :]
    valid = valid3.reshape(N, 1)
    zr = zr3.reshape(N, cols)
    m = jnp.where(valid, zr + (prev3.reshape(N, cols) - zr) * mu_ref[...], 0.0)

    r = m[:, 0:W]
    k = m[:, W:2 * W]
    v = m[:, 2 * W:3 * W]
    x2 = m[:, 3 * W:3 * W + LANES]
    gd = m[:, 3 * W + LANES:3 * W + 2 * LANES]

    wpre = w0_ref[...] + _bdot(jnp.tanh(x2), w2_ref[...])
    nw = -wpre
    softplus = jnp.maximum(nw, 0.0) + jnp.log(1.0 + jnp.exp(-jnp.abs(nw)))
    lw = jnp.where(valid, -jnp.exp(-softplus - 0.5), 0.0)
    a = _sigmoid(a0_ref[...] + _bdot(x2, a2_ref[...]))
    g = _bdot(_sigmoid(gd), g2_ref[...])
    if layer0:
        vf_out_ref[...] = v.reshape(nb, C, W)
    else:
        vf = jnp.where(valid, vf_ref[...].reshape(N, W), 0.0)
        v = v + (vf - v) * _sigmoid(v0_ref[...] + _bdot(_bdot(v, v1_ref[...]), v2_ref[...]))

    lane = lax.broadcasted_iota(jnp.int32, (2 * LANES, 2 * LANES), 1)
    sub = lax.broadcasted_iota(jnp.int32, (2 * LANES, 2 * LANES), 0)
    seg = (lane // RW_HEAD == sub // RW_HEAD).astype(BF16)

    kkv = k * kk_ref[...]
    kkn = kkv * lax.rsqrt(jnp.maximum(_head_sums(kkv * kkv, seg), 1e-24))
    k2 = k * (1.0 + (a - 1.0) * ka_ref[...])
    aa = -kkn
    bb = kkn * a
    bonus = _head_sums(r * k2 * rk_ref[...], seg)

    tr = lax.broadcasted_iota(jnp.int32, (N, N), 0)
    tc = lax.broadcasted_iota(jnp.int32, (N, N), 1)
    tri = jnp.logical_and(tr // C == tc // C, tr >= tc).astype(BF16)
    cl = _dot_hilo(tri, lw)
    cl3 = cl.reshape(nb, C, W)
    cl_end3 = cl3[:, C - 1:C, :]
    p_in = jnp.exp(cl)
    p_ex = jnp.exp(cl - lw)
    p_inv = jnp.exp(-cl)
    p_end = jnp.exp(cl_end3 - cl3).reshape(N, W)
    p_tot3 = jnp.exp(cl_end3)

    SW = hps // 2 * LANES
    lane_c = lax.broadcasted_iota(jnp.int32, (1, C, SW), 2)
    m1 = (lane_c % LANES < RW_HEAD).astype(BF16)
    m2 = (lane_c % LANES >= RW_HEAD).astype(BF16)

    def slab(x, i):
        b, p = divmod(i, RW_PAIRS)
        return x[b * C:(b + 1) * C, p * LANES:(p + 1) * LANES]

    def pairs(x):
        if packed:
            return jnp.stack([jnp.concatenate([slab(x, g), slab(x, g + G)], axis=1) for g in range(G)], axis=0)
        return jnp.stack([slab(x, g) for g in range(G)], axis=0)

    def stack(x):
        xp = pairs(x).astype(BF16)
        return jnp.concatenate([xp * m1, xp * m2], axis=1)

    def widen(mask):
        return jnp.concatenate([mask] * (hps // 2), axis=1)

    rr = lax.broadcasted_iota(jnp.int32, (2 * C, 2 * C), 0)
    cc = lax.broadcasted_iota(jnp.int32, (2 * C, 2 * C), 1)
    same = (rr // C) == (cc // C)
    strict = widen(jnp.logical_and(same, (rr % C) > (cc % C)))
    incl = widen(jnp.logical_and(same, (rr % C) >= (cc % C)))
    base = 8
    eye = widen((rr == cc).astype(F32))
    base_mask = widen(jnp.logical_and(rr // base == cc // base, rr > cc))
    merge_masks = []
    kb = base
    while kb < C:
        merge_masks.append(widen(jnp.logical_and(rr // (2 * kb) == cc // (2 * kb), rr // kb == cc // kb + 1)))
        kb *= 2

    def bdiag(x):
        lo, hi = x[..., :LANES], x[..., LANES:]
        z = jnp.zeros_like(lo)
        return jnp.concatenate([jnp.concatenate([lo, z], axis=2), jnp.concatenate([z, hi], axis=2)], axis=1)

    def mm(a_, b_):
        return _bmm(a_, bdiag(b_.astype(BF16))) if packed else _bmm(a_, b_)

    s_bd = sbd_scr[...]
    ar = jnp.concatenate([stack(aa * p_ex), stack(r * p_in)], axis=1)
    bk = jnp.concatenate([stack(bb * p_inv), stack(k2 * p_inv)], axis=1)
    vm = stack(v)
    mats = [_bmm_nt(ar[..., h * LANES:(h + 1) * LANES], bk[..., h * LANES:(h + 1) * LANES])
            for h in range(hps // 2)]
    blk = lambda i, j: jnp.concatenate([mt[:, i * 2 * C:(i + 1) * 2 * C, j * 2 * C:(j + 1) * 2 * C] for mt in mats],
                                       axis=2)
    l_ab = jnp.where(strict, blk(0, 0), 0.0)
    a_ak = jnp.where(strict, blk(0, 1), 0.0)
    a_rb = jnp.where(incl, blk(1, 0), 0.0)
    a_rk = jnp.where(incl, blk(1, 1), 0.0)
    gs = _bmm_nt(ar, s_bd)
    inv = _tril_inverse(l_ab, eye, base_mask, merge_masks, mm)
    x = mm(inv, gs[:, :2 * C] + mm(a_ak, vm))
    ym = gs[:, 2 * C:] + mm(a_rb, x) + mm(a_rk, vm)
    yp = ym[:, :C] + ym[:, C:]
    bkh = jnp.concatenate([stack(bb * p_end), stack(k2 * p_end)], axis=1)
    uv = jnp.concatenate([x, vm.astype(F32)], axis=1)
    upd = _bmm(jnp.stack([uv[i].T for i in range(G)], axis=0), bkh)
    if packed:
        z = jnp.zeros((G, LANES, LANES), F32)
        upd = jnp.concatenate([jnp.concatenate([upd[:, :LANES, :LANES], z], axis=2),
                               jnp.concatenate([z, upd[:, LANES:, LANES:]], axis=2)], axis=1)
    tot = lambda i: p_tot3[i // RW_PAIRS, :, (i % RW_PAIRS) * LANES:(i % RW_PAIRS + 1) * LANES]
    p_tot = jnp.stack([jnp.concatenate([tot(g), tot(g + G)], axis=1) if packed else tot(g) for g in range(G)], axis=0)
    s_new = s_bd * p_tot + upd
    sbd_scr[...] = s_new

    def y_slab(i):
        return yp[i % G][:, (i // G) * LANES:(i // G + 1) * LANES]

    y = jnp.concatenate(
        [jnp.concatenate([y_slab(b * RW_PAIRS + p) for p in range(RW_PAIRS)], axis=1) for b in range(nb)], axis=0)

    inv_n = 1.0 / RW_HEAD
    mean = _head_sums(y, seg) * inv_n
    d = y - mean
    var = _head_sums(d * d, seg) * inv_n
    yn = d * lax.rsqrt(var + LNX_EPS) * lnw_ref[...] + lnb_ref[...]
    y_ref[...] = ((yn + bonus * v) * g).reshape(nb, C, W).astype(y_ref.dtype)

    @pl.when(c_idx == n_chunks - 1)
    def _():
        last = (seq - 1) % C
        shout_ref[...] = zr3[:, last:last + 1, :]
        for g in range(G):
            for n_, (b, h) in enumerate(slot_heads(g)):
                sout_ref[b, h] = s_new[g][n_ * RW_HEAD:(n_ + 1) * RW_HEAD, n_ * RW_HEAD:(n_ + 1) * RW_HEAD]


def _rwkv(zr, shift0, wkv0, v_first, prm, *, chunk, seq, nb, layer0):
    b, lpad, cols = zr.shape
    assert b % nb == 0
    n_chunks = pl.cdiv(lpad, chunk)
    W = RW_WIDTH
    row = lambda n: pl.BlockSpec((1, n), lambda bi, c: (0, 0))
    full = lambda s: pl.BlockSpec(s, lambda bi, c: (0, 0))
    tok = lambda n: pl.BlockSpec((nb, chunk, n), lambda bi, c: (bi, c, 0))
    state = pl.BlockSpec((nb, RW_HEADS, RW_HEAD, RW_HEAD), lambda bi, c: (bi, 0, 0, 0))
    shift = pl.BlockSpec((nb, 1, cols), lambda bi, c: (bi, 0, 0))
    in_specs = [tok(cols), shift, state]
    args = [zr, shift0.reshape(b, 1, cols), wkv0]
    if not layer0:
        in_specs.append(tok(W))
        args.append(v_first)
    in_specs += [row(cols), row(W), full((LANES, W)), row(W), full((LANES, W)), full((GATE_LORA, W))]
    args += [prm['mu'], prm['w0'], prm['w2'], prm['a0'], prm['a2'], prm['g2']]
    if not layer0:
        in_specs += [row(W), full((W, LANES)), full((LANES, W))]
        args += [prm['v0'], prm['v1'], prm['v2']]
    in_specs += [row(W)] * 5
    args += [prm['k_k'], prm['k_a'], prm['r_k'], prm['ln_w'], prm['ln_b']]
    out_shape = [jax.ShapeDtypeStruct((b, lpad, W), BF16),
                 jax.ShapeDtypeStruct((b, RW_HEADS, RW_HEAD, RW_HEAD), F32),
                 jax.ShapeDtypeStruct((b, 1, cols), F32)]
    out_specs = [tok(W), state, shift]
    if layer0:
        out_shape.append(jax.ShapeDtypeStruct((b, lpad, W), F32))
        out_specs.append(tok(W))
    outs = pl.pallas_call(
        functools.partial(_rwkv_kernel, chunk=chunk, seq=seq, layer0=layer0),
        grid=(b // nb, n_chunks),
        in_specs=in_specs,
        out_specs=out_specs,
        out_shape=out_shape,
        scratch_shapes=[pltpu.VMEM((nb * RW_PAIRS // 2, 2 * LANES, 2 * LANES) if 2 * chunk == LANES
                                   else (nb * RW_PAIRS, LANES, LANES), F32),
                        pltpu.VMEM((nb, chunk + SUBLANES, cols), F32)],
        compiler_params=_params(("parallel", "arbitrary")),
        name="rwkv",
    )(*args)
    if layer0:
        y, s_out, sh_out, vf = outs
    else:
        (y, s_out, sh_out), vf = outs, v_first
    return y, s_out, sh_out.reshape(b, cols), vf


def _rms(x, g):
    ms = jnp.mean(x * x, axis=-1, keepdims=True)
    return x * lax.rsqrt(ms + 1e-6) * g


def _merge_kernel(att_ref, rw_ref, ga_ref, gb_ref, x_ref, wa_ref, wb_ref, wo_ref, n2_ref, o_ref):
    ya = _bdot(att_ref[...], wa_ref[...])
    yb = _bdot(rw_ref[...], wb_ref[...])
    merged = _sigmoid(ga_ref[...].astype(F32)) * ya + _sigmoid(gb_ref[...].astype(F32)) * yb
    mix = _bdot(merged, wo_ref[...])
    o_ref[...] = x_ref[...] + _rms(mix, n2_ref[...])


def _merge(att, rw, gates, x, wa, wb, wo, n2, *, tm):
    t, d = x.shape
    assert t % tm == 0
    tokb = lambda j: pl.BlockSpec((tm, d), lambda i: (i, j))
    wspec = pl.BlockSpec((d, d), lambda i: (0, 0))
    return pl.pallas_call(
        _merge_kernel,
        grid=(t // tm,),
        in_specs=[tokb(0), tokb(0), tokb(0), tokb(1), tokb(0), wspec, wspec, wspec,
                  pl.BlockSpec((1, d), lambda i: (0, 0))],
        out_specs=tokb(0),
        out_shape=jax.ShapeDtypeStruct((t, d), F32),
        compiler_params=_params(("parallel",)),
        name="merge",
    )(att, rw, gates, gates, x, wa, wb, wo, n2.reshape(1, d))


def _ffn_kernel(x_ref, n3_ref, wu_ref, wd_ref, n4_ref, o_ref, *, n_split):
    x = x_ref[...]
    h = _rms(x, n3_ref[...]).astype(BF16)
    dff = wu_ref.shape[1]
    step = dff // n_split
    f = jnp.zeros(x.shape, F32)
    for s in range(n_split):
        u = jnp.dot(h, wu_ref[:, s * step:(s + 1) * step], preferred_element_type=F32)
        u = jnp.square(jnp.maximum(u, 0.0))
        f = f + jnp.dot(u.astype(BF16), wd_ref[s * step:(s + 1) * step, :], preferred_element_type=F32)
    o_ref[...] = x + _rms(f, n4_ref[...])


def _ffn(x, n3, wu, wd, n4, *, tm):
    t, d = x.shape
    dff = wu.shape[1]
    assert t % tm == 0
    tokb = pl.BlockSpec((tm, d), lambda i: (i, 0))
    vec = pl.BlockSpec((1, d), lambda i: (0, 0))
    return pl.pallas_call(
        functools.partial(_ffn_kernel, n_split=4),
        grid=(t // tm,),
        in_specs=[tokb, vec, pl.BlockSpec((d, dff), lambda i: (0, 0)), pl.BlockSpec((dff, d), lambda i: (0, 0)), vec],
        out_specs=tokb,
        out_shape=jax.ShapeDtypeStruct((t, d), F32),
        compiler_params=_params(("parallel",)),
        name="ffn",
    )(x, n3.reshape(1, d), wu, wd, n4.reshape(1, d))


def _pick_tile(n, candidates):
    for c in candidates:
        if n % c == 0:
            return c
    return n


def kernel(x_prompt, x_sample, cache_k, cache_v, state_wkv, state_shift, page_table, meta_tokens, norm_pre_mix, norm_post_mix, norm_pre_ffn, norm_post_ffn, w_in, lambda_q1, lambda_k1, lambda_q2, lambda_k2, da_subln, rw_mu, rw_w0, rw_w2, rw_a0, rw_a2, rw_g2, rw_v0, rw_v1, rw_v2, rw_k_k, rw_k_a, rw_r_k, rw_ln_w, rw_ln_b, w_branch_a, w_branch_b, w_out, w_up, w_down):
    depth = w_in.shape[0]
    b, seq_in, d = x_prompt.shape
    db, ds, _ = x_sample.shape
    seq = seq_in + N_META
    n_past = page_table.shape[1] * PAGE_SIZE
    qk_w = DA_MAPS * DA_HEAD_DIM
    da_w = DA_HEADS * DA_VALUE_DIM
    assert ds <= SUBLANES

    xp = jnp.concatenate([jnp.broadcast_to(meta_tokens[None], (b, N_META, d)), x_prompt], axis=1)
    xs = x_sample.reshape(1, db * ds, d)
    tabs_p = _rope_tables(jnp.arange(seq))
    tabs_s = _rope_tables(n_past + jnp.arange(db * ds) % ds)
    tm_p = _pick_tile(seq, (688, 344, 72, 48, 16))
    tm_p2 = _pick_tile(seq, (344, 72, 48, 16))
    tm_s = _pick_tile(db * ds, (512, 256, 128, 64, 32, 16, 8))
    tm_flat_p = _pick_tile(b * seq, (384, 256, 128, 64, 48, 32, 16, 8))
    attn_tq = 128
    chunk_p = 64
    chunk_s = SUBLANES
    nb_s = _pick_tile(db, (8, 4, 2, 1))
    nb_p = _pick_tile(b, (2, 1))

    cache_kt = cache_k.transpose(0, 1, 3, 4, 2)
    cache_v2 = cache_v.reshape(depth, cache_v.shape[1], PAGE_SIZE * DA_HEADS, DA_VALUE_DIM)
    w_in_b = w_in.astype(BF16)
    zeros_lora = jnp.zeros((DECAY_LORA, RW_WIDTH), BF16)
    wkv0_p = jnp.zeros((b, RW_HEADS, RW_HEAD, RW_HEAD), F32)
    shift0_p = jnp.zeros((b, RW_COLS), F32)
    q_scale = DA_HEAD_DIM ** -0.5 * LOG2E
    pad_tok = lambda t: jnp.pad(t, ((0, 0), (0, SUBLANES - ds)) + ((0, 0),) * (t.ndim - 2))

    vf_p = vf_s = None
    outs = [[] for _ in range(8)]
    for l in range(depth):
        o = 0
        wq = w_in_b[l, :, o:o + qk_w]; o += qk_w
        wk = w_in_b[l, :, o:o + qk_w]; o += qk_w
        wv = w_in_b[l, :, o:o + da_w]; o += da_w
        wz = w_in_b[l, :, o:o + RW_COLS]; o += RW_COLS
        wg = w_in_b[l, :, o:o + 2 * d]
        row = lambda t: t.reshape(1, -1).astype(F32)
        prm = {'mu': row(rw_mu[l]), 'w0': row(rw_w0[l]), 'a0': row(rw_a0[l]),
               'w2': jnp.concatenate([rw_w2[l].astype(BF16), zeros_lora], axis=0),
               'a2': jnp.concatenate([zeros_lora, rw_a2[l].astype(BF16)], axis=0),
               'g2': rw_g2[l].astype(BF16),
               'k_k': row(rw_k_k[l]), 'k_a': row(rw_k_a[l]), 'r_k': row(rw_r_k[l]),
               'ln_w': row(rw_ln_w[l]), 'ln_b': row(rw_ln_b[l])}
        if l > 0:
            prm['v0'] = row(rw_v0[l - 1])
            prm['v1'] = jnp.pad(rw_v1[l - 1].astype(BF16), ((0, 0), (0, LANES - MV_LORA)))
            prm['v2'] = jnp.pad(rw_v2[l - 1].astype(BF16), ((0, LANES - MV_LORA), (0, 0)))
        lam_init = 0.8 - 0.6 * math.exp(-0.3 * l)
        lam_vecs = [t[l].reshape(1, DA_HEAD_DIM).astype(F32) for t in (lambda_q1, lambda_k1, lambda_q2, lambda_k2)]
        wa, wb, wo = w_branch_a[l].astype(BF16), w_branch_b[l].astype(BF16), w_out[l].astype(BF16)
        wu, wd = w_up[l].astype(BF16), w_down[l].astype(BF16)
        n1 = norm_pre_mix[l]

        def project(x3, tm, tm_wide, tabs, q_dtype):
            q, k, v = _norm_matmul(x3, n1, [wq, wk, wv], tm=tm, out_dtypes=(q_dtype, F32, F32),
                                   ropes=(True, True, False), scales=(q_scale, 1.0, 1.0), rope_tabs=tabs)
            zr, gt = _norm_matmul(x3, n1, [wz, wg], tm=tm_wide, out_dtypes=(F32, BF16),
                                  ropes=(False, False), scales=(1.0, 1.0), rope_tabs=None)
            return q, k, v, zr, gt

        q, k, v, zr, gt = project(xp, tm_p, tm_p2, tabs_p, BF16)
        att = _attn_prompt(q, k, v, lam_vecs, da_subln[l], lam_init, tq=attn_tq)
        rw, wkv_p, sh_p, vf_p = _rwkv(zr, shift0_p, wkv0_p, vf_p, prm, chunk=chunk_p, seq=seq, nb=nb_p,
                                      layer0=(l == 0))
        flat = lambda t: t.reshape(b * seq, -1)
        x2 = _merge(flat(att), flat(rw), flat(gt), flat(xp), wa, wb, wo, norm_post_mix[l], tm=tm_flat_p)
        x2 = _ffn(x2, norm_pre_ffn[l], wu, wd, norm_post_ffn[l], tm=tm_flat_p)
        xp = x2.reshape(b, seq, d)
        outs[0].append(k.reshape(b, seq, DA_MAPS, DA_HEAD_DIM))
        outs[1].append(v.reshape(b, seq, DA_HEADS, DA_VALUE_DIM))
        outs[2].append(wkv_p)
        outs[3].append(sh_p)

        q, k, v, zr, gt = project(xs, tm_s, tm_s, tabs_s, F32)
        k4 = k.reshape(db, ds, DA_MAPS, DA_HEAD_DIM)
        v4 = v.reshape(db, ds, DA_HEADS, DA_VALUE_DIM)
        q4 = pad_tok(q.reshape(db, ds, DA_MAPS, DA_HEAD_DIM)).transpose(0, 2, 1, 3)
        att8 = _attn_sample(l, q4, cache_kt, cache_v2, page_table, pad_tok(k4), pad_tok(v4), lam_vecs,
                            da_subln[l], lam_init, ds)
        att = att8[:, :ds].reshape(db * ds, da_w)
        rw8, wkv_s, sh_s, vf_s = _rwkv(pad_tok(zr.reshape(db, ds, RW_COLS)), state_shift[l], state_wkv[l], vf_s,
                                        prm, chunk=chunk_s, seq=ds, nb=nb_s, layer0=(l == 0))
        rw = rw8[:, :ds].reshape(db * ds, RW_WIDTH)
        x2 = _merge(att, rw, gt.reshape(db * ds, -1), xs.reshape(db * ds, d), wa, wb, wo, norm_post_mix[l], tm=tm_s)
        x2 = _ffn(x2, norm_pre_ffn[l], wu, wd, norm_post_ffn[l], tm=tm_s)
        xs = x2.reshape(1, db * ds, d)
        outs[4].append(k4)
        outs[5].append(v4)
        outs[6].append(wkv_s)
        outs[7].append(sh_s)

    y_prompt = xp[:, N_META:]
    y_sample = xs.reshape(db, ds, d)
    return (y_prompt, y_sample) + tuple(jnp.stack(o) for o in outs)
```
